```python
import jax, jax.numpy as jnp
from jax import lax
import numpy as np

D_MODEL = 1024
BATCH = 32
SEQ = 2048
DEPTH = 4

PLE_DIM = 256
EXPAND = 2
D_MIX = EXPAND * D_MODEL
D_HGRN = D_MIX // 2
D_FOX = D_MIX - D_HGRN
HGRN_HEAD_DIM = 128
HGRN_HEADS = D_HGRN // HGRN_HEAD_DIM
FOX_HEAD_DIM = 64
FOX_HEADS = D_FOX // FOX_HEAD_DIM
HGRN_CHUNK = 32
FOX_BLOCK = 128
NORM_EPS = 1e-6
D_IN = 4 * D_HGRN + 4 * D_FOX + FOX_HEADS
SPLIT_POINTS = tuple(D_HGRN * j for j in range(1, 5)) + tuple(4 * D_HGRN + D_FOX * j for j in range(1, 5))

kernel_name = "hymba_style_hgrn2_fox_hybrid"


def rmsnorm(x, w):
    xf = x.astype(jnp.float32)
    y = xf * lax.rsqrt(jnp.mean(xf * xf, axis=-1, keepdims=True) + NORM_EPS)
    return (y * w.astype(jnp.float32)).astype(x.dtype)


def hgrn_lower_bounds(logits):
    sm = jax.nn.softmax(logits.astype(jnp.float32), axis=0)
    c = jnp.cumsum(sm, axis=0)
    return c - c[0:1]


def hgrn2_branch(q, fz, i, lb, gn_w):
    B, S, _ = q.shape
    H, dk, C = HGRN_HEADS, HGRN_HEAD_DIM, HGRN_CHUNK
    N = S // C
    f32 = jnp.float32
    fz = fz.astype(f32)
    lb = lb.astype(f32)
    log_f = jnp.logaddexp(jnp.log(lb), jnp.log1p(-lb) + jax.nn.log_sigmoid(fz))
    k = (1.0 - lb) * jax.nn.sigmoid(-fz)

    def to_chunks(t):
        return t.astype(f32).reshape(B, N, C, H, -1).transpose(1, 0, 3, 2, 4)

    qc = to_chunks(q) * (dk ** -0.5)
    kc, vc, gc = to_chunks(k), to_chunks(i), to_chunks(log_f)
    causal = jnp.tril(jnp.ones((C, C), dtype=bool))[:, :, None]

    def step(state, xs):
        qb, kb, vb, gb = xs
        b = jnp.cumsum(gb, axis=2)
        diff = b[:, :, :, None, :] - b[:, :, None, :, :]
        decay = jnp.exp(jnp.where(causal, diff, -jnp.inf))
        scores = jnp.einsum('bhtd,bhsd,bhtsd->bhts', qb, kb, decay)
        o = (jnp.einsum('bhts,bhsv->bhtv', scores, vb)
             + jnp.einsum('bhtd,bhdv->bhtv', qb * jnp.exp(b), state))
        b_last = b[:, :, -1:, :]
        new_state = (jnp.exp(b_last[:, :, 0, :])[..., None] * state
                     + jnp.einsum('bhsd,bhsv->bhdv', kb * jnp.exp(b_last - b), vb))
        return new_state, o

    state0 = jnp.zeros((B, H, dk, vc.shape[-1]), f32)
    _, o = lax.scan(step, state0, (qc, kc, vc, gc))
    o = o.transpose(1, 0, 3, 2, 4).reshape(B, S, H, -1)
    o = o * lax.rsqrt(jnp.mean(o * o, axis=-1, keepdims=True) + NORM_EPS)
    o = o * gn_w.astype(f32).reshape(H, -1)
    return o.reshape(B, S, D_HGRN).astype(q.dtype)


def fox_branch(q, k, v, fz, fb):
    B, S, _ = q.shape
    H, d, Q = FOX_HEADS, FOX_HEAD_DIM, FOX_BLOCK
    f32 = jnp.float32

    def heads(t):
        return t.reshape(B, S, H, d).transpose(0, 2, 1, 3)

    q, k, v = heads(q), heads(k), heads(v)
    log_f = jax.nn.log_sigmoid(fz.astype(f32) + fb.astype(f32))
    F = jnp.cumsum(log_f, axis=1).transpose(0, 2, 1)
    scale = d ** -0.5
    outs = []
    for blk in range(S // Q):
        q0, q1 = blk * Q, (blk + 1) * Q
        qb = q[:, :, q0:q1]
        kb, vb = k[:, :, :q1], v[:, :, :q1]
        logits = (jnp.einsum('bhtd,bhsd->bhts', qb, kb).astype(f32) * scale
                  + F[:, :, q0:q1, None] - F[:, :, None, :q1])
        mask = jnp.arange(q0, q1)[:, None] >= jnp.arange(q1)[None, :]
        probs = jax.nn.softmax(jnp.where(mask, logits, -jnp.inf), axis=-1)
        outs.append(jnp.einsum('bhts,bhsd->bhtd', probs.astype(v.dtype), vb))
    o = jnp.concatenate(outs, axis=2)
    return o.transpose(0, 2, 1, 3).reshape(B, S, D_FOX)


def setup_inputs(seed: int = 0) -> dict:
    key = jax.random.key(seed)
    ks = jax.random.split(key, 11)
    f32 = jnp.float32
    x = jax.random.normal(ks[0], (BATCH, SEQ, D_MODEL), f32)
    p = jax.random.normal(ks[1], (DEPTH, BATCH, SEQ, PLE_DIM), f32)
    norm_w = 1.0 + 0.1 * jax.random.normal(ks[2], (DEPTH, D_MODEL), f32)
    w_in = jax.random.normal(ks[3], (DEPTH, D_MODEL, D_IN), f32) * D_MODEL ** -0.5
    fox_fb = jax.random.uniform(ks[4], (DEPTH, FOX_HEADS), f32, 1.0, 4.0)
    hgrn_gn = 1.0 + 0.1 * jax.random.normal(ks[5], (DEPTH, D_HGRN), f32)
    hgrn_lb_logits = 0.5 * jax.random.normal(ks[6], (DEPTH, D_HGRN), f32)
    w_out = jax.random.normal(ks[7], (DEPTH, D_MIX, D_MODEL), f32) * D_MIX ** -0.5
    w_ple = jax.random.normal(ks[8], (DEPTH, PLE_DIM, D_MODEL), f32) * PLE_DIM ** -0.5
    w_ple_gate = jax.random.normal(ks[9], (DEPTH, D_MODEL, D_MODEL), f32) * D_MODEL ** -0.5
    final_norm_w = 1.0 + 0.1 * jax.random.normal(ks[10], (D_MODEL,), f32)
    return {"x": x, "p": p, "norm_w": norm_w, "w_in": w_in, "fox_fb": fox_fb,
            "hgrn_gn": hgrn_gn, "hgrn_lb_logits": hgrn_lb_logits, "w_out": w_out,
            "w_ple": w_ple, "w_ple_gate": w_ple_gate, "final_norm_w": final_norm_w}


def reference(x, p, norm_w, w_in, fox_fb, hgrn_gn, hgrn_lb_logits, w_out, w_ple, w_ple_gate, final_norm_w):
    lb_all = hgrn_lower_bounds(hgrn_lb_logits)
    h = x
    for i in range(DEPTH):
        u = rmsnorm(h, norm_w[i])
        proj = jnp.einsum('bsd,de->bse', u, w_in[i])
        hq, hf, hi, hg, fq, fk, fv, fg, fz = jnp.split(proj, SPLIT_POINTS, axis=-1)
        y_h = hgrn2_branch(hq, hf, hi, lb_all[i], hgrn_gn[i]) * jax.nn.silu(hg)
        y_f = fox_branch(fq, fk, fv, fz, fox_fb[i]) * jax.nn.silu(fg)
        y = jnp.concatenate([y_h, y_f], axis=-1)
        h = h + jnp.einsum('bse,ed->bsd', y, w_out[i])
        ple = jnp.einsum('bsk,kd->bsd', p[i], w_ple[i])
        h = h + jax.nn.sigmoid(jnp.einsum('bsd,de->bse', h, w_ple_gate[i])) * ple
    return rmsnorm(h, final_norm_w)
```

```python
import functools

import jax
import jax.numpy as jnp
from jax import lax
from jax.experimental import pallas as pl
from jax.experimental.pallas import tpu as pltpu

D_MODEL = 1024
PLE_DIM = 256
D_HGRN = 1024
D_FOX = 1024
HGRN_HEAD_DIM = 128
HGRN_HEADS = D_HGRN // HGRN_HEAD_DIM
FOX_HEAD_DIM = 64
FOX_HEADS = D_FOX // FOX_HEAD_DIM
NORM_EPS = 1e-6
D_MAIN = 4 * D_HGRN + 4 * D_FOX

LANES = 128
HGRN_SUB = 32
HGRN_SLAB = 128
FOX_T = 128
FOX_SLOT = 8
HGRN_FAST_MIN_LOG_DECAY = -60.0
VMEM_LIMIT = 48 * 1024 * 1024

F32 = jnp.float32
BF16 = jnp.bfloat16


def _dot(a, b):
    return jnp.dot(a, b, preferred_element_type=F32)


def _dot_nt(a, b):
    return lax.dot_general(a, b, (((1,), (1,)), ((), ())), preferred_element_type=F32)


def _dot_tn(a, b):
    return lax.dot_general(a, b, (((0,), (0,)), ((), ())), preferred_element_type=F32)


def _split_bf16(x, n):
    parts = []
    r = x
    for _ in range(n - 1):
        hi = r.astype(BF16)
        parts.append(hi)
        r = r - hi.astype(F32)
    parts.append(r.astype(BF16))
    return parts


def _dot_01_lhs(m01, x, n):
    acc = None
    for part in _split_bf16(x, n):
        t = _dot(m01, part)
        acc = t if acc is None else acc + t
    return acc


def _log_sigmoid(z):
    return jnp.minimum(z, 0.0) - jnp.log1p(jnp.exp(-jnp.abs(z)))


def _inproj_kernel(x_ref, nw_ref, w_ref, wz_ref, proj_ref, fz_ref, u_ref):
    @pl.when(pl.program_id(1) == 0)
    def _():
        x = x_ref[...]
        ms = jnp.mean(x * x, axis=-1, keepdims=True)
        u = (x * lax.rsqrt(ms + NORM_EPS) * nw_ref[...]).astype(BF16)
        u_ref[...] = u
        fz_ref[...] = _dot(u, wz_ref[...])

    proj_ref[...] = _dot(u_ref[...], w_ref[...]).astype(BF16)


def _inproj(h, nw, w_main, w_gate, tm=1024, tn=2048):
    m = h.shape[0]
    return pl.pallas_call(
        _inproj_kernel,
        grid=(m // tm, D_MAIN // tn),
        in_specs=[
            pl.BlockSpec((tm, D_MODEL), lambda i, j: (i, 0)),
            pl.BlockSpec((1, D_MODEL), lambda i, j: (0, 0)),
            pl.BlockSpec((D_MODEL, tn), lambda i, j: (0, j)),
            pl.BlockSpec((D_MODEL, LANES), lambda i, j: (0, 0)),
        ],
        out_specs=[
            pl.BlockSpec((tm, tn), lambda i, j: (i, j)),
            pl.BlockSpec((tm, LANES), lambda i, j: (i, 0)),
        ],
        out_shape=[
            jax.ShapeDtypeStruct((m, D_MAIN), BF16),
            jax.ShapeDtypeStruct((m, LANES), F32),
        ],
        scratch_shapes=[pltpu.VMEM((tm, D_MODEL), BF16)],
        compiler_params=pltpu.CompilerParams(
            dimension_semantics=("parallel", "arbitrary"), vmem_limit_bytes=VMEM_LIMIT),
        name="inproj",
    )(h, nw, w_main, w_gate)


def _hgrn_kernel(layer, q_ref, f_ref, i_ref, g_ref, lbl_ref, gn_ref, o_ref,
                 qs_ref, ks_ref, bs_ref, vs_ref, oi_ref):
    seq = q_ref.shape[1]
    n_sub = HGRN_SLAB // HGRN_SUB

    if layer > 0:
        logits = lbl_ref[...]
        ex = jnp.exp(logits - jnp.max(logits, axis=0, keepdims=True))
        sm = ex / jnp.sum(ex, axis=0, keepdims=True)
        lb = jnp.sum(sm[1:layer + 1], axis=0, keepdims=True)
        log_lb = jnp.log(lb)
        log_1m_lb = jnp.log1p(-lb)
        one_m_lb = 1.0 - lb

    row = lax.broadcasted_iota(jnp.int32, (HGRN_SLAB, HGRN_SLAB), 0)
    col = lax.broadcasted_iota(jnp.int32, (HGRN_SLAB, HGRN_SLAB), 1)
    causal = (row >= col) & ((row // HGRN_SUB) == (col // HGRN_SUB))
    tril = jnp.where(causal, 1.0, 0.0).astype(BF16)
    gn = gn_ref[...]
    scale = HGRN_HEAD_DIM ** -0.5

    def slab(si, state_t):
        r0 = pl.multiple_of(si * HGRN_SLAB, HGRN_SLAB)
        rows = pl.ds(r0, HGRN_SLAB)
        fz = f_ref[0, rows, :].astype(F32)
        e = jnp.exp(-jnp.abs(fz))
        log_sig = jnp.minimum(fz, 0.0) - jnp.log1p(e)
        r = 1.0 / (1.0 + e)
        sig_neg = jnp.where(fz >= 0.0, e * r, r)
        if layer > 0:
            bb = log_1m_lb + log_sig
            log_f = jnp.maximum(log_lb, bb) + jnp.log1p(jnp.exp(-jnp.abs(log_lb - bb)))
            k = one_m_lb * sig_neg
        else:
            log_f = log_sig
            k = sig_neg

        b = _dot_01_lhs(tril, log_f, 2)
        b_last = jnp.concatenate(
            [jnp.broadcast_to(b[(c + 1) * HGRN_SUB - 1:(c + 1) * HGRN_SUB, :], (HGRN_SUB, HGRN_HEAD_DIM))
             for c in range(n_sub)], axis=0)
        q = q_ref[0, rows, :].astype(F32) * scale
        v = i_ref[0, rows, :]
        qf = (q * jnp.exp(b)).astype(BF16)
        kb = (k * jnp.exp(b_last - b)).astype(BF16)

        def intra_fast():
            kf = (k * jnp.exp(-b)).astype(BF16)
            scores = jnp.where(causal, _dot_nt(qf, kf), 0.0)
            return _dot(scores.astype(BF16), v)

        def intra_direct():
            qs_ref[...] = q
            ks_ref[...] = k
            bs_ref[...] = b
            vs_ref[...] = v.astype(F32)
            sub_row = lax.broadcasted_iota(jnp.int32, (HGRN_SUB, 1), 0)

            def one_row(t, carry):
                c0 = pl.multiple_of((t // HGRN_SUB) * HGRN_SUB, HGRN_SUB)
                crow = pl.ds(c0, HGRN_SUB)
                d = bs_ref[pl.ds(t, 1), :] - bs_ref[crow, :]
                keep = (sub_row + c0) <= t
                w = jnp.where(keep, jnp.exp(jnp.minimum(d, 0.0)), 0.0)
                sc = jnp.sum(qs_ref[pl.ds(t, 1), :] * ks_ref[crow, :] * w, axis=1, keepdims=True)
                oi_ref[pl.ds(t, 1), :] = jnp.sum(sc * vs_ref[crow, :], axis=0, keepdims=True)
                return carry

            lax.fori_loop(0, HGRN_SLAB, one_row, 0)
            return oi_ref[...]

        o_intra = lax.cond(jnp.min(b) >= HGRN_FAST_MIN_LOG_DECAY, intra_fast, intra_direct)

        o_inter = []
        for c in range(n_sub):
            sl = slice(c * HGRN_SUB, (c + 1) * HGRN_SUB)
            o_inter.append(_dot_nt(qf[sl], state_t.astype(BF16)))
            decay = jnp.exp(b[(c + 1) * HGRN_SUB - 1:(c + 1) * HGRN_SUB, :])
            state_t = state_t * decay + _dot_tn(v[sl], kb[sl])
        o = o_intra + jnp.concatenate(o_inter, axis=0)

        o = o * lax.rsqrt(jnp.mean(o * o, axis=-1, keepdims=True) + NORM_EPS) * gn
        g = g_ref[0, rows, :].astype(F32)
        o_ref[0, rows, :] = (o * g * (1.0 / (1.0 + jnp.exp(-g)))).astype(BF16)
        return state_t

    lax.fori_loop(0, seq // HGRN_SLAB, slab, jnp.zeros((HGRN_HEAD_DIM, HGRN_HEAD_DIM), F32))


def _hgrn(layer, proj, lb_logits, gn):
    bsz, seq, _ = proj.shape
    depth = lb_logits.shape[0]
    nh = HGRN_HEADS

    def col(off):
        return pl.BlockSpec((1, seq, HGRN_HEAD_DIM), lambda b, h: (b, 0, off + h))

    return pl.pallas_call(
        functools.partial(_hgrn_kernel, layer),
        grid=(bsz, nh),
        in_specs=[col(0), col(nh), col(2 * nh), col(3 * nh),
                  pl.BlockSpec((depth, HGRN_HEAD_DIM), lambda b, h: (0, h)),
                  pl.BlockSpec((1, HGRN_HEAD_DIM), lambda b, h: (0, h))],
        out_specs=pl.BlockSpec((1, seq, HGRN_HEAD_DIM), lambda b, h: (b, 0, h)),
        out_shape=jax.ShapeDtypeStruct((bsz, seq, D_HGRN), BF16),
        scratch_shapes=[pltpu.VMEM((HGRN_SLAB, HGRN_HEAD_DIM), F32) for _ in range(5)],
        compiler_params=pltpu.CompilerParams(
            dimension_semantics=("parallel", "parallel"), vmem_limit_bytes=VMEM_LIMIT),
        name="hgrn2",
    )(proj, proj, proj, proj, lb_logits, gn)


def _foxprep_kernel(fz_ref, fb_ref, fa_ref, fbm_ref):
    seq = fz_ref.shape[1]
    blk = LANES
    row = lax.broadcasted_iota(jnp.int32, (blk, blk), 0)
    col = lax.broadcasted_iota(jnp.int32, (blk, blk), 1)
    tril = jnp.where(row >= col, 1.0, 0.0).astype(BF16)
    sel = [jnp.where((col == FOX_SLOT * row + j) & (row < FOX_HEADS), 1.0, 0.0).astype(BF16)
           for j in range(6)]
    lane = lax.broadcasted_iota(jnp.int32, (1, blk), 1) % FOX_SLOT
    ones_a = jnp.where((lane >= 3) & (lane < 6), 1.0, 0.0)
    ones_b = jnp.where(lane < 3, 1.0, 0.0)
    fb = fb_ref[...]
    carry = jnp.zeros((1, blk), F32)
    for i in range(seq // blk):
        rows = slice(i * blk, (i + 1) * blk)
        log_f = _log_sigmoid(fz_ref[0, rows, :] + fb)
        csum = _dot_01_lhs(tril, log_f, 3) + carry
        carry = csum[blk - 1:blk, :]
        parts = _split_bf16(csum, 3)
        fa = ones_a
        fbm = ones_b
        for j in range(3):
            fa = fa + _dot(parts[j], sel[j])
            fbm = fbm - _dot(parts[j], sel[3 + j])
        fa_ref[0, rows, :] = fa.astype(BF16)
        fbm_ref[0, rows, :] = fbm.astype(BF16)


def _foxprep(fz, fb):
    bsz, seq, _ = fz.shape
    spec = pl.BlockSpec((1, seq, LANES), lambda b: (b, 0, 0))
    return pl.pallas_call(
        _foxprep_kernel,
        grid=(bsz,),
        in_specs=[spec, pl.BlockSpec((1, LANES), lambda b: (0, 0))],
        out_specs=[spec, spec],
        out_shape=[jax.ShapeDtypeStruct((bsz, seq, LANES), BF16)] * 2,
        compiler_params=pltpu.CompilerParams(
            dimension_semantics=("parallel",), vmem_limit_bytes=VMEM_LIMIT),
        name="foxprep",
    )(fz, fb)


def _fox_kernel(q_ref, k_ref, v_ref, g_ref, fa_ref, fbm_ref, o_ref):
    seq = q_ref.shape[1]
    t = FOX_T
    pair = pl.program_id(1)
    lane = lax.broadcasted_iota(jnp.int32, (1, LANES), 1)
    row = lax.broadcasted_iota(jnp.int32, (t, t), 0)
    col = lax.broadcasted_iota(jnp.int32, (t, t), 1)
    diag_keep = row >= col
    neg = -1e30
    qscale = jnp.asarray(FOX_HEAD_DIM ** -0.5, BF16)

    def qblock(qi, carry):
        r0 = pl.multiple_of(qi * t, t)
        rows = pl.ds(r0, t)
        q = q_ref[0, rows, :]
        fa = fa_ref[0, rows, :]
        out = jnp.zeros((t, LANES), F32)
        for hh in range(2):
            hmask = (lane // FOX_HEAD_DIM) == hh
            fmask = (lane // FOX_SLOT) == (2 * pair + hh)
            zero = jnp.zeros((), BF16)
            qa = jnp.concatenate(
                [jnp.where(hmask, q, zero) * qscale, jnp.where(fmask, fa, zero)], axis=1)

            def step(c0, m, l, acc, masked):
                cols = pl.ds(c0, t)
                ka = jnp.concatenate([k_ref[0, cols, :], fbm_ref[0, cols, :]], axis=1)
                s = _dot_nt(qa, ka)
                if masked:
                    s = jnp.where(diag_keep, s, neg)
                m_new = jnp.maximum(m, jnp.max(s, axis=-1, keepdims=True))
                p = jnp.exp(s - m_new)
                alpha = jnp.exp(m - m_new)
                l = alpha * l + jnp.sum(p, axis=-1, keepdims=True)
                acc = alpha * acc + _dot(p.astype(BF16), v_ref[0, cols, :])
                return m_new, l, acc

            def kblock(kj, mla):
                return step(pl.multiple_of(kj * t, t), *mla, masked=False)

            init = (jnp.full((t, 1), neg, F32), jnp.zeros((t, 1), F32), jnp.zeros((t, LANES), F32))
            m, l, acc = lax.fori_loop(0, qi, kblock, init)
            m, l, acc = step(r0, m, l, acc, masked=True)
            out = jnp.where(hmask, acc / l, out)
        g = g_ref[0, rows, :].astype(F32)
        o_ref[0, rows, :] = (out * g * (1.0 / (1.0 + jnp.exp(-g)))).astype(BF16)
        return carry

    lax.fori_loop(0, seq // t, qblock, 0)


def _fox(proj, fa, fbm):
    bsz, seq, _ = proj.shape
    npair = D_FOX // LANES
    base = 4 * D_HGRN // LANES

    def col(j):
        return pl.BlockSpec((1, seq, LANES), lambda b, h: (b, 0, base + j * npair + h))

    gate = pl.BlockSpec((1, seq, LANES), lambda b, h: (b, 0, 0))
    return pl.pallas_call(
        _fox_kernel,
        grid=(bsz, npair),
        in_specs=[col(0), col(1), col(2), col(3), gate, gate],
        out_specs=pl.BlockSpec((1, seq, LANES), lambda b, h: (b, 0, h)),
        out_shape=jax.ShapeDtypeStruct((bsz, seq, D_FOX), BF16),
        compiler_params=pltpu.CompilerParams(
            dimension_semantics=("parallel", "parallel"), vmem_limit_bytes=VMEM_LIMIT),
        name="fox",
    )(proj, proj, proj, proj, fa, fbm)


def _outproj_kernel(final, yh_ref, yf_ref, h_ref, p_ref, woh_ref, wof_ref, wp_ref, wg_ref,
                    fnw_ref, o_ref):
    h = h_ref[...] + _dot(yh_ref[...], woh_ref[...]) + _dot(yf_ref[...], wof_ref[...])
    ple = _dot(p_ref[...].astype(BF16), wp_ref[...])
    gate = _dot(h.astype(BF16), wg_ref[...])
    h = h + (1.0 / (1.0 + jnp.exp(-gate))) * ple
    if final:
        h = h * lax.rsqrt(jnp.mean(h * h, axis=-1, keepdims=True) + NORM_EPS) * fnw_ref[...]
    o_ref[...] = h


def _outproj(final, yh, yf, h, p, w_out_h, w_out_f, w_ple, w_gate, fnw, tm=512):
    m = h.shape[0]

    def rows(width):
        return pl.BlockSpec((tm, width), lambda i: (i, 0))

    def whole(a):
        return pl.BlockSpec(a.shape, lambda i: (0, 0))

    return pl.pallas_call(
        functools.partial(_outproj_kernel, final),
        grid=(m // tm,),
        in_specs=[rows(D_HGRN), rows(D_FOX), rows(D_MODEL), rows(PLE_DIM),
                  whole(w_out_h), whole(w_out_f), whole(w_ple), whole(w_gate), whole(fnw)],
        out_specs=rows(D_MODEL),
        out_shape=jax.ShapeDtypeStruct((m, D_MODEL), F32),
        compiler_params=pltpu.CompilerParams(
            dimension_semantics=("parallel",), vmem_limit_bytes=VMEM_LIMIT),
        name="outproj",
    )(yh, yf, h, p, w_out_h, w_out_f, w_ple, w_gate, fnw)


def kernel(x, p, norm_w, w_in, fox_fb, hgrn_gn, hgrn_lb_logits, w_out, w_ple, w_ple_gate, final_norm_w):
    bsz, seq, _ = x.shape
    depth = w_in.shape[0]
    m = bsz * seq
    w_main = w_in[:, :, :D_MAIN].astype(BF16)
    w_fgate = jnp.pad(w_in[:, :, D_MAIN:], ((0, 0), (0, 0), (0, LANES - FOX_HEADS))).astype(BF16)
    fb = jnp.pad(fox_fb, ((0, 0), (0, LANES - FOX_HEADS)))
    w_out_b = w_out.astype(BF16)
    w_ple_b = w_ple.astype(BF16)
    w_gate_b = w_ple_gate.astype(BF16)
    fnw = final_norm_w.reshape(1, D_MODEL)

    h = x.reshape(m, D_MODEL)
    for i in range(depth):
        proj, fz = _inproj(h, norm_w[i].reshape(1, D_MODEL), w_main[i], w_fgate[i])
        proj = proj.reshape(bsz, seq, D_MAIN)
        yh = _hgrn(i, proj, hgrn_lb_logits, hgrn_gn[i].reshape(1, D_HGRN))
        fa, fbm = _foxprep(fz.reshape(bsz, seq, LANES), fb[i].reshape(1, LANES))
        yf = _fox(proj, fa, fbm)
        h = _outproj(i == depth - 1, yh.reshape(m, D_HGRN), yf.reshape(m, D_FOX), h,
                     p[i].reshape(m, PLE_DIM), w_out_b[i, :D_HGRN], w_out_b[i, D_HGRN:],
                     w_ple_b[i], w_gate_b[i], fnw)
    return h.reshape(bsz, seq, D_MODEL)
```

```python
import functools

import jax
import jax.numpy as jnp
from jax import lax
from jax.experimental import pallas as pl
from jax.experimental.pallas import tpu as pltpu

D_MODEL = 1024
PLE_DIM = 256
D_HGRN = 1024
D_FOX = 1024
HGRN_HEAD_DIM = 128
HGRN_HEADS = D_HGRN // HGRN_HEAD_DIM
FOX_HEAD_DIM = 64
FOX_HEADS = D_FOX // FOX_HEAD_DIM
NORM_EPS = 1e-6
D_MAIN = 4 * D_HGRN + 4 * D_FOX

LANES = 128
HGRN_SUB = 32
HGRN_SLAB = 128
FOX_TQ = 256
FOX_SLOT = 8
HGRN_FAST_MIN_LOG_DECAY = -60.0
VMEM_LIMIT = 48 * 1024 * 1024

F32 = jnp.float32
BF16 = jnp.bfloat16


def _dot(a, b):
    return jnp.dot(a, b, preferred_element_type=F32)


def _dot_nt(a, b):
    return lax.dot_general(a, b, (((1,), (1,)), ((), ())), preferred_element_type=F32)


def _split_bf16(x, n):
    parts = []
    r = x
    for _ in range(n - 1):
        hi = r.astype(BF16)
        parts.append(hi)
        r = r - hi.astype(F32)
    parts.append(r.astype(BF16))
    return parts


def _dot_01_lhs(m01, x, n):
    acc = None
    for part in _split_bf16(x, n):
        t = _dot(m01, part)
        acc = t if acc is None else acc + t
    return acc


def _log_sigmoid(z):
    return jnp.minimum(z, 0.0) - jnp.log1p(jnp.exp(-jnp.abs(z)))


def _inproj_kernel(x_ref, nw_ref, w_ref, wz_ref, proj_ref, fz_ref, u_ref):
    @pl.when(pl.program_id(1) == 0)
    def _():
        x = x_ref[...]
        ms = jnp.mean(x * x, axis=-1, keepdims=True)
        u = (x * lax.rsqrt(ms + NORM_EPS) * nw_ref[...]).astype(BF16)
        u_ref[...] = u
        fz_ref[...] = _dot(u, wz_ref[...])

    proj_ref[...] = _dot(u_ref[...], w_ref[...]).astype(BF16)


def _inproj(h, nw, w_main, w_gate, tm=1024, tn=2048):
    m = h.shape[0]
    return pl.pallas_call(
        _inproj_kernel,
        grid=(m // tm, D_MAIN // tn),
        in_specs=[
            pl.BlockSpec((tm, D_MODEL), lambda i, j: (i, 0)),
            pl.BlockSpec((1, D_MODEL), lambda i, j: (0, 0)),
            pl.BlockSpec((D_MODEL, tn), lambda i, j: (0, j)),
            pl.BlockSpec((D_MODEL, LANES), lambda i, j: (0, 0)),
        ],
        out_specs=[
            pl.BlockSpec((tm, tn), lambda i, j: (i, j)),
            pl.BlockSpec((tm, LANES), lambda i, j: (i, 0)),
        ],
        out_shape=[
            jax.ShapeDtypeStruct((m, D_MAIN), BF16),
            jax.ShapeDtypeStruct((m, LANES), F32),
        ],
        scratch_shapes=[pltpu.VMEM((tm, D_MODEL), BF16)],
        compiler_params=pltpu.CompilerParams(
            dimension_semantics=("parallel", "arbitrary"), vmem_limit_bytes=VMEM_LIMIT),
        name="inproj",
    )(h, nw, w_main, w_gate)


def _hgrn_kernel(layer, q_ref, f_ref, i_ref, g_ref, lbl_ref, gn_ref, o_ref,
                 qf_ref, oi_ref, ds_ref, dec_ref, qs_ref, ks_ref, bs_ref, vs_ref, flag_ref):
    seq = q_ref.shape[1]
    n_slab = seq // HGRN_SLAB
    n_sub = HGRN_SLAB // HGRN_SUB
    n_chunk = seq // HGRN_SUB

    if layer > 0:
        logits = lbl_ref[...]
        ex = jnp.exp(logits - jnp.max(logits, axis=0, keepdims=True))
        sm = ex / jnp.sum(ex, axis=0, keepdims=True)
        lb = jnp.sum(sm[1:layer + 1], axis=0, keepdims=True)
        log_lb = jnp.log(lb)
        log_1m_lb = jnp.log1p(-lb)
        one_m_lb = 1.0 - lb

    row = lax.broadcasted_iota(jnp.int32, (HGRN_SLAB, HGRN_SLAB), 0)
    col = lax.broadcasted_iota(jnp.int32, (HGRN_SLAB, HGRN_SLAB), 1)
    causal = (row >= col) & ((row // HGRN_SUB) == (col // HGRN_SUB))
    tril = jnp.where(causal, 1.0, 0.0).astype(BF16)
    row_chunk = lax.broadcasted_iota(jnp.int32, (HGRN_SLAB, HGRN_HEAD_DIM), 0) // HGRN_SUB
    scale = HGRN_HEAD_DIM ** -0.5

    def slab_rows(si):
        return pl.ds(pl.multiple_of(si * HGRN_SLAB, HGRN_SLAB), HGRN_SLAB)

    def decays(rows):
        fz = f_ref[0, rows, :].astype(F32)
        e = jnp.exp(-jnp.abs(fz))
        log_sig = jnp.minimum(fz, 0.0) - jnp.log1p(e)
        r = 1.0 / (1.0 + e)
        sig_neg = jnp.where(fz >= 0.0, e * r, r)
        if layer > 0:
            bb = log_1m_lb + log_sig
            log_f = jnp.maximum(log_lb, bb) + jnp.log1p(jnp.exp(-jnp.abs(log_lb - bb)))
            k = one_m_lb * sig_neg
        else:
            log_f = log_sig
            k = sig_neg
        return k, _dot_01_lhs(tril, log_f, 2)

    def pass1(si, carry):
        rows = slab_rows(si)
        k, b = decays(rows)
        b_last = jnp.concatenate(
            [jnp.broadcast_to(b[(c + 1) * HGRN_SUB - 1:(c + 1) * HGRN_SUB, :], (HGRN_SUB, HGRN_HEAD_DIM))
             for c in range(n_sub)], axis=0)
        q = q_ref[0, rows, :].astype(F32) * scale
        v = i_ref[0, rows, :]
        qf = (q * jnp.exp(b)).astype(BF16)
        kf = (k * jnp.exp(-b)).astype(BF16)
        kb = (k * jnp.exp(b_last - b)).astype(BF16)
        scores = jnp.where(causal, _dot_nt(qf, kf), 0.0)
        oi_ref[rows, :] = _dot(scores.astype(BF16), v)
        qf_ref[rows, :] = qf
        v_t = v.astype(F32).T.astype(BF16)
        zero = jnp.zeros((), BF16)
        for c in range(n_sub):
            ci = si * n_sub + c
            ds_ref[ci] = _dot(v_t, jnp.where(row_chunk == c, kb, zero))
            dec_ref[pl.ds(ci, 1), :] = jnp.exp(b[(c + 1) * HGRN_SUB - 1:(c + 1) * HGRN_SUB, :])
        flag_ref[si] = (jnp.min(b) < HGRN_FAST_MIN_LOG_DECAY).astype(jnp.int32)
        return carry

    lax.fori_loop(0, n_slab, pass1, 0, unroll=2)

    def redo_direct(si, carry):
        @pl.when(flag_ref[si] != 0)
        def _():
            rows = slab_rows(si)
            r0 = pl.multiple_of(si * HGRN_SLAB, HGRN_SLAB)
            k, b = decays(rows)
            qs_ref[...] = q_ref[0, rows, :].astype(F32) * scale
            ks_ref[...] = k
            bs_ref[...] = b
            vs_ref[...] = i_ref[0, rows, :].astype(F32)
            sub_row = lax.broadcasted_iota(jnp.int32, (HGRN_SUB, 1), 0)

            def one_row(t, c):
                c0 = pl.multiple_of((t // HGRN_SUB) * HGRN_SUB, HGRN_SUB)
                crow = pl.ds(c0, HGRN_SUB)
                d = bs_ref[pl.ds(t, 1), :] - bs_ref[crow, :]
                keep = (sub_row + c0) <= t
                w = jnp.where(keep, jnp.exp(jnp.minimum(d, 0.0)), 0.0)
                sc = jnp.sum(qs_ref[pl.ds(t, 1), :] * ks_ref[crow, :] * w, axis=1, keepdims=True)
                oi_ref[pl.ds(r0 + t, 1), :] = jnp.sum(sc * vs_ref[crow, :], axis=0, keepdims=True)
                return c

            lax.fori_loop(0, HGRN_SLAB, one_row, 0)
        return carry

    lax.fori_loop(0, n_slab, redo_direct, 0)

    def pass2(ci, state_t):
        crow = pl.ds(pl.multiple_of(ci * HGRN_SUB, HGRN_SUB), HGRN_SUB)
        oi_ref[crow, :] += _dot_nt(qf_ref[crow, :], state_t.astype(BF16))
        return state_t * dec_ref[pl.ds(ci, 1), :] + ds_ref[ci]

    lax.fori_loop(0, n_chunk, pass2, jnp.zeros((HGRN_HEAD_DIM, HGRN_HEAD_DIM), F32), unroll=4)

    gn = gn_ref[...]

    def pass3(si, carry):
        rows = slab_rows(si)
        o = oi_ref[rows, :]
        o = o * lax.rsqrt(jnp.mean(o * o, axis=-1, keepdims=True) + NORM_EPS) * gn
        g = g_ref[0, rows, :].astype(F32)
        o_ref[0, rows, :] = (o * g * (1.0 / (1.0 + jnp.exp(-g)))).astype(BF16)
        return carry

    lax.fori_loop(0, n_slab, pass3, 0, unroll=2)


def _hgrn(layer, proj, lb_logits, gn):
    bsz, seq, _ = proj.shape
    depth = lb_logits.shape[0]
    nh = HGRN_HEADS

    def col(off):
        return pl.BlockSpec((1, seq, HGRN_HEAD_DIM), lambda b, h: (b, 0, off + h))

    return pl.pallas_call(
        functools.partial(_hgrn_kernel, layer),
        grid=(bsz, nh),
        in_specs=[col(0), col(nh), col(2 * nh), col(3 * nh),
                  pl.BlockSpec((depth, HGRN_HEAD_DIM), lambda b, h: (0, h)),
                  pl.BlockSpec((1, HGRN_HEAD_DIM), lambda b, h: (0, h))],
        out_specs=pl.BlockSpec((1, seq, HGRN_HEAD_DIM), lambda b, h: (b, 0, h)),
        out_shape=jax.ShapeDtypeStruct((bsz, seq, D_HGRN), BF16),
        scratch_shapes=[
            pltpu.VMEM((seq, HGRN_HEAD_DIM), BF16),
            pltpu.VMEM((seq, HGRN_HEAD_DIM), F32),
            pltpu.VMEM((seq // HGRN_SUB, HGRN_HEAD_DIM, HGRN_HEAD_DIM), F32),
            pltpu.VMEM((seq // HGRN_SUB, HGRN_HEAD_DIM), F32),
            pltpu.VMEM((HGRN_SLAB, HGRN_HEAD_DIM), F32),
            pltpu.VMEM((HGRN_SLAB, HGRN_HEAD_DIM), F32),
            pltpu.VMEM((HGRN_SLAB, HGRN_HEAD_DIM), F32),
            pltpu.VMEM((HGRN_SLAB, HGRN_HEAD_DIM), F32),
            pltpu.SMEM((seq // HGRN_SLAB,), jnp.int32),
        ],
        compiler_params=pltpu.CompilerParams(
            dimension_semantics=("parallel", "parallel"), vmem_limit_bytes=VMEM_LIMIT),
        name="hgrn2",
    )(proj, proj, proj, proj, lb_logits, gn)


def _foxprep_kernel(fz_ref, fb_ref, fa_ref, fbm_ref):
    seq = fz_ref.shape[1]
    blk = LANES
    row = lax.broadcasted_iota(jnp.int32, (blk, blk), 0)
    col = lax.broadcasted_iota(jnp.int32, (blk, blk), 1)
    tril = jnp.where(row >= col, 1.0, 0.0).astype(BF16)
    sel = [jnp.where((col == FOX_SLOT * row + j) & (row < FOX_HEADS), 1.0, 0.0).astype(BF16)
           for j in range(6)]
    lane = lax.broadcasted_iota(jnp.int32, (1, blk), 1) % FOX_SLOT
    ones_a = jnp.where((lane >= 3) & (lane < 6), 1.0, 0.0)
    ones_b = jnp.where(lane < 3, 1.0, 0.0)
    fb = fb_ref[...]
    carry = jnp.zeros((1, blk), F32)
    for i in range(seq // blk):
        rows = slice(i * blk, (i + 1) * blk)
        log_f = _log_sigmoid(fz_ref[0, rows, :] + fb)
        csum = _dot_01_lhs(tril, log_f, 3) + carry
        carry = csum[blk - 1:blk, :]
        parts = _split_bf16(csum, 3)
        fa = ones_a
        fbm = ones_b
        for j in range(3):
            fa = fa + _dot(parts[j], sel[j])
            fbm = fbm - _dot(parts[j], sel[3 + j])
        fa_ref[0, rows, :] = fa.astype(BF16)
        fbm_ref[0, rows, :] = fbm.astype(BF16)


def _foxprep(fz, fb):
    bsz, seq, _ = fz.shape
    spec = pl.BlockSpec((1, seq, LANES), lambda b: (b, 0, 0))
    return pl.pallas_call(
        _foxprep_kernel,
        grid=(bsz,),
        in_specs=[spec, pl.BlockSpec((1, LANES), lambda b: (0, 0))],
        out_specs=[spec, spec],
        out_shape=[jax.ShapeDtypeStruct((bsz, seq, LANES), BF16)] * 2,
        compiler_params=pltpu.CompilerParams(
            dimension_semantics=("parallel",), vmem_limit_bytes=VMEM_LIMIT),
        name="foxprep",
    )(fz, fb)


def _fox_kernel(q_ref, k_ref, v_ref, g_ref, fa_ref, fbm_ref, o_ref, ka_ref, vt_ref):
    seq = q_ref.shape[1]
    tq = FOX_TQ
    pair = pl.program_id(1)
    lane = lax.broadcasted_iota(jnp.int32, (1, LANES), 1)
    key_i = lax.broadcasted_iota(jnp.int32, (tq, tq), 0)
    qry_i = lax.broadcasted_iota(jnp.int32, (tq, tq), 1)
    diag_keep = key_i <= qry_i
    neg = -1e30
    qscale = jnp.asarray(FOX_HEAD_DIM ** -0.5, BF16)
    zero = jnp.zeros((), BF16)

    ka_ref[:, :LANES] = k_ref[0]
    ka_ref[:, LANES:] = fbm_ref[0]
    vt_ref[...] = v_ref[0].astype(F32).T.astype(BF16)

    for j in range(seq // tq):
        rows = slice(j * tq, (j + 1) * tq)
        n_off = j * tq
        q = q_ref[0, rows, :]
        fa = fa_ref[0, rows, :]
        outs = []
        for hh in range(2):
            hmask = (lane // FOX_HEAD_DIM) == hh
            fmask = (lane // FOX_SLOT) == (2 * pair + hh)
            qa = jnp.concatenate(
                [jnp.where(hmask, q, zero) * qscale, jnp.where(fmask, fa, zero)], axis=1)
            vt_h = slice(hh * FOX_HEAD_DIM, (hh + 1) * FOX_HEAD_DIM)
            s_diag = jnp.where(diag_keep, _dot_nt(ka_ref[rows, :], qa), neg)
            m = jnp.max(s_diag, axis=0, keepdims=True)
            if n_off:
                s_off = _dot_nt(ka_ref[:n_off, :], qa)
                m = jnp.maximum(m, jnp.max(s_off, axis=0, keepdims=True))
            p_diag = jnp.exp(s_diag - m)
            l = jnp.sum(p_diag, axis=0, keepdims=True)
            o_t = _dot(vt_ref[vt_h, rows], p_diag.astype(BF16))
            if n_off:
                p_off = jnp.exp(s_off - m)
                l = l + jnp.sum(p_off, axis=0, keepdims=True)
                o_t = o_t + _dot(vt_ref[vt_h, :n_off], p_off.astype(BF16))
            outs.append(o_t * (1.0 / l))
        o = jnp.concatenate(outs, axis=0).T
        g = g_ref[0, rows, :].astype(F32)
        o_ref[0, rows, :] = (o * g * (1.0 / (1.0 + jnp.exp(-g)))).astype(BF16)


def _fox(proj, fa, fbm):
    bsz, seq, _ = proj.shape
    npair = D_FOX // LANES
    base = 4 * D_HGRN // LANES

    def col(j):
        return pl.BlockSpec((1, seq, LANES), lambda b, h: (b, 0, base + j * npair + h))

    gate = pl.BlockSpec((1, seq, LANES), lambda b, h: (b, 0, 0))
    return pl.pallas_call(
        _fox_kernel,
        grid=(bsz, npair),
        in_specs=[col(0), col(1), col(2), col(3), gate, gate],
        out_specs=pl.BlockSpec((1, seq, LANES), lambda b, h: (b, 0, h)),
        out_shape=jax.ShapeDtypeStruct((bsz, seq, D_FOX), BF16),
        scratch_shapes=[pltpu.VMEM((seq, 2 * LANES), BF16), pltpu.VMEM((LANES, seq), BF16)],
        compiler_params=pltpu.CompilerParams(
            dimension_semantics=("parallel", "parallel"), vmem_limit_bytes=VMEM_LIMIT),
        name="fox",
    )(proj, proj, proj, proj, fa, fbm)


def _outproj_kernel(final, yh_ref, yf_ref, h_ref, p_ref, woh_ref, wof_ref, wp_ref, wg_ref,
                    fnw_ref, o_ref):
    h = h_ref[...] + _dot(yh_ref[...], woh_ref[...]) + _dot(yf_ref[...], wof_ref[...])
    ple = _dot(p_ref[...].astype(BF16), wp_ref[...])
    gate = _dot(h.astype(BF16), wg_ref[...])
    h = h + (1.0 / (1.0 + jnp.exp(-gate))) * ple
    if final:
        h = h * lax.rsqrt(jnp.mean(h * h, axis=-1, keepdims=True) + NORM_EPS) * fnw_ref[...]
    o_ref[...] = h


def _outproj(final, yh, yf, h, p, w_out_h, w_out_f, w_ple, w_gate, fnw, tm=512):
    m = h.shape[0]

    def rows(width):
        return pl.BlockSpec((tm, width), lambda i: (i, 0))

    def whole(a):
        return pl.BlockSpec(a.shape, lambda i: (0, 0))

    return pl.pallas_call(
        functools.partial(_outproj_kernel, final),
        grid=(m // tm,),
        in_specs=[rows(D_HGRN), rows(D_FOX), rows(D_MODEL), rows(PLE_DIM),
                  whole(w_out_h), whole(w_out_f), whole(w_ple), whole(w_gate), whole(fnw)],
        out_specs=rows(D_MODEL),
        out_shape=jax.ShapeDtypeStruct((m, D_MODEL), F32),
        compiler_params=pltpu.CompilerParams(
            dimension_semantics=("parallel",), vmem_limit_bytes=VMEM_LIMIT),
        name="outproj",
    )(yh, yf, h, p, w_out_h, w_out_f, w_ple, w_gate, fnw)


def kernel(x, p, norm_w, w_in, fox_fb, hgrn_gn, hgrn_lb_logits, w_out, w_ple, w_ple_gate, final_norm_w):
    bsz, seq, _ = x.shape
    depth = w_in.shape[0]
    m = bsz * seq
    w_main = w_in[:, :, :D_MAIN].astype(BF16)
    w_fgate = jnp.pad(w_in[:, :, D_MAIN:], ((0, 0), (0, 0), (0, LANES - FOX_HEADS))).astype(BF16)
    fb = jnp.pad(fox_fb, ((0, 0), (0, LANES - FOX_HEADS)))
    w_out_b = w_out.astype(BF16)
    w_ple_b = w_ple.astype(BF16)
    w_gate_b = w_ple_gate.astype(BF16)
    fnw = final_norm_w.reshape(1, D_MODEL)

    h = x.reshape(m, D_MODEL)
    for i in range(depth):
        proj, fz = _inproj(h, norm_w[i].reshape(1, D_MODEL), w_main[i], w_fgate[i])
        proj = proj.reshape(bsz, seq, D_MAIN)
        yh = _hgrn(i, proj, hgrn_lb_logits, hgrn_gn[i].reshape(1, D_HGRN))
        fa, fbm = _foxprep(fz.reshape(bsz, seq, LANES), fb[i].reshape(1, LANES))
        yf = _fox(proj, fa, fbm)
        h = _outproj(i == depth - 1, yh.reshape(m, D_HGRN), yf.reshape(m, D_FOX), h,
                     p[i].reshape(m, PLE_DIM), w_out_b[i, :D_HGRN], w_out_b[i, D_HGRN:],
                     w_ple_b[i], w_gate_b[i], fnw)
    return h.reshape(bsz, seq, D_MODEL)
```

```python
import functools

import jax
import jax.numpy as jnp
from jax import lax
from jax.experimental import pallas as pl
from jax.experimental.pallas import tpu as pltpu

D_MODEL = 1024
PLE_DIM = 256
D_HGRN = 1024
D_FOX = 1024
HGRN_HEAD_DIM = 128
HGRN_HEADS = D_HGRN // HGRN_HEAD_DIM
FOX_HEAD_DIM = 64
FOX_HEADS = D_FOX // FOX_HEAD_DIM
NORM_EPS = 1e-6
D_MAIN = 4 * D_HGRN + 4 * D_FOX

LANES = 128
HGRN_SUB = 32
HGRN_SLAB = 128
FOX_TQ = 256
FOX_SLOT = 8
HGRN_FAST_MIN_LOG_DECAY = -60.0
VMEM_LIMIT = 48 * 1024 * 1024

F32 = jnp.float32
BF16 = jnp.bfloat16


def _dot(a, b):
    return jnp.dot(a, b, preferred_element_type=F32)


def _dot_nt(a, b):
    return lax.dot_general(a, b, (((1,), (1,)), ((), ())), preferred_element_type=F32)


def _split_bf16(x, n):
    parts = []
    r = x
    for _ in range(n - 1):
        hi = r.astype(BF16)
        parts.append(hi)
        r = r - hi.astype(F32)
    parts.append(r.astype(BF16))
    return parts


def _dot_01_lhs(m01, x, n):
    acc = None
    for part in _split_bf16(x, n):
        t = _dot(m01, part)
        acc = t if acc is None else acc + t
    return acc


def _log_sigmoid(z):
    return jnp.minimum(z, 0.0) - jnp.log1p(jnp.exp(-jnp.abs(z)))


def _inproj_kernel(x_ref, nw_ref, w_ref, wz_ref, proj_ref, fz_ref, u_ref):
    @pl.when(pl.program_id(1) == 0)
    def _():
        x = x_ref[...]
        ms = jnp.mean(x * x, axis=-1, keepdims=True)
        u = (x * lax.rsqrt(ms + NORM_EPS) * nw_ref[...]).astype(BF16)
        u_ref[...] = u
        fz_ref[...] = _dot(u, wz_ref[...])

    proj_ref[...] = _dot(u_ref[...], w_ref[...]).astype(BF16)


def _inproj(h, nw, w_main, w_gate, tm=1024, tn=2048):
    m = h.shape[0]
    return pl.pallas_call(
        _inproj_kernel,
        grid=(m // tm, D_MAIN // tn),
        in_specs=[
            pl.BlockSpec((tm, D_MODEL), lambda i, j: (i, 0)),
            pl.BlockSpec((1, D_MODEL), lambda i, j: (0, 0)),
            pl.BlockSpec((D_MODEL, tn), lambda i, j: (0, j)),
            pl.BlockSpec((D_MODEL, LANES), lambda i, j: (0, 0)),
        ],
        out_specs=[
            pl.BlockSpec((tm, tn), lambda i, j: (i, j)),
            pl.BlockSpec((tm, LANES), lambda i, j: (i, 0)),
        ],
        out_shape=[
            jax.ShapeDtypeStruct((m, D_MAIN), BF16),
            jax.ShapeDtypeStruct((m, LANES), F32),
        ],
        scratch_shapes=[pltpu.VMEM((tm, D_MODEL), BF16)],
        compiler_params=pltpu.CompilerParams(
            dimension_semantics=("parallel", "arbitrary"), vmem_limit_bytes=VMEM_LIMIT),
        name="inproj",
    )(h, nw, w_main, w_gate)


def _hgrn_kernel(layer, q_ref, f_ref, i_ref, g_ref, lbl_ref, gn_ref, o_ref,
                 qf_ref, oi_ref, ds_ref, dec_ref, qs_ref, ks_ref, bs_ref, vs_ref, flag_ref):
    seq = q_ref.shape[1]
    n_slab = seq // HGRN_SLAB
    n_sub = HGRN_SLAB // HGRN_SUB
    n_chunk = seq // HGRN_SUB

    if layer > 0:
        logits = lbl_ref[...]
        ex = jnp.exp(logits - jnp.max(logits, axis=0, keepdims=True))
        sm = ex / jnp.sum(ex, axis=0, keepdims=True)
        lb = jnp.sum(sm[1:layer + 1], axis=0, keepdims=True)
        log_lb = jnp.log(lb)
        log_1m_lb = jnp.log1p(-lb)
        one_m_lb = 1.0 - lb

    row = lax.broadcasted_iota(jnp.int32, (HGRN_SLAB, HGRN_SLAB), 0)
    col = lax.broadcasted_iota(jnp.int32, (HGRN_SLAB, HGRN_SLAB), 1)
    causal = (row >= col) & ((row // HGRN_SUB) == (col // HGRN_SUB))
    tril = jnp.where(causal, 1.0, 0.0).astype(BF16)
    row_chunk = lax.broadcasted_iota(jnp.int32, (HGRN_SLAB, HGRN_HEAD_DIM), 0) // HGRN_SUB
    scale = HGRN_HEAD_DIM ** -0.5

    def slab_rows(si):
        return pl.ds(pl.multiple_of(si * HGRN_SLAB, HGRN_SLAB), HGRN_SLAB)

    def decays(rows):
        fz = f_ref[0, rows, :].astype(F32)
        e = jnp.exp(-jnp.abs(fz))
        log_sig = jnp.minimum(fz, 0.0) - jnp.log1p(e)
        r = 1.0 / (1.0 + e)
        sig_neg = jnp.where(fz >= 0.0, e * r, r)
        if layer > 0:
            bb = log_1m_lb + log_sig
            log_f = jnp.maximum(log_lb, bb) + jnp.log1p(jnp.exp(-jnp.abs(log_lb - bb)))
            k = one_m_lb * sig_neg
        else:
            log_f = log_sig
            k = sig_neg
        hi, lo = _split_bf16(log_f, 2)
        csum = _dot(tril, jnp.concatenate([hi, lo], axis=1))
        return k, csum[:, :HGRN_HEAD_DIM] + csum[:, HGRN_HEAD_DIM:]

    def pass1(si, carry):
        rows = slab_rows(si)
        k, b = decays(rows)
        b_last = jnp.concatenate(
            [jnp.broadcast_to(b[(c + 1) * HGRN_SUB - 1:(c + 1) * HGRN_SUB, :], (HGRN_SUB, HGRN_HEAD_DIM))
             for c in range(n_sub)], axis=0)
        q = q_ref[0, rows, :].astype(F32) * scale
        v = i_ref[0, rows, :]
        qf = (q * jnp.exp(b)).astype(BF16)
        kf = (k * jnp.exp(-b)).astype(BF16)
        kb = (k * jnp.exp(b_last - b)).astype(BF16)
        scores = jnp.where(causal, _dot_nt(qf, kf), 0.0)
        oi_ref[rows, :] = _dot(scores.astype(BF16), v)
        qf_ref[rows, :] = qf
        v_t = v.astype(F32).T.astype(BF16)
        zero = jnp.zeros((), BF16)
        for c in range(0, n_sub, 2):
            kb2 = jnp.concatenate([jnp.where(row_chunk == c, kb, zero),
                                   jnp.where(row_chunk == c + 1, kb, zero)], axis=1)
            ds2 = _dot(v_t, kb2)
            ds_ref[si * n_sub + c] = ds2[:, :HGRN_HEAD_DIM]
            ds_ref[si * n_sub + c + 1] = ds2[:, HGRN_HEAD_DIM:]
        for c in range(n_sub):
            ci = si * n_sub + c
            dec_ref[pl.ds(ci, 1), :] = jnp.exp(b[(c + 1) * HGRN_SUB - 1:(c + 1) * HGRN_SUB, :])
        flag_ref[si] = (jnp.min(b) < HGRN_FAST_MIN_LOG_DECAY).astype(jnp.int32)
        return carry

    lax.fori_loop(0, n_slab, pass1, 0, unroll=4)

    def redo_direct(si, carry):
        @pl.when(flag_ref[si] != 0)
        def _():
            rows = slab_rows(si)
            r0 = pl.multiple_of(si * HGRN_SLAB, HGRN_SLAB)
            k, b = decays(rows)
            qs_ref[...] = q_ref[0, rows, :].astype(F32) * scale
            ks_ref[...] = k
            bs_ref[...] = b
            vs_ref[...] = i_ref[0, rows, :].astype(F32)
            sub_row = lax.broadcasted_iota(jnp.int32, (HGRN_SUB, 1), 0)

            def one_row(t, c):
                c0 = pl.multiple_of((t // HGRN_SUB) * HGRN_SUB, HGRN_SUB)
                crow = pl.ds(c0, HGRN_SUB)
                d = bs_ref[pl.ds(t, 1), :] - bs_ref[crow, :]
                keep = (sub_row + c0) <= t
                w = jnp.where(keep, jnp.exp(jnp.minimum(d, 0.0)), 0.0)
                sc = jnp.sum(qs_ref[pl.ds(t, 1), :] * ks_ref[crow, :] * w, axis=1, keepdims=True)
                oi_ref[pl.ds(r0 + t, 1), :] = jnp.sum(sc * vs_ref[crow, :], axis=0, keepdims=True)
                return c

            lax.fori_loop(0, HGRN_SLAB, one_row, 0)
        return carry

    lax.fori_loop(0, n_slab, redo_direct, 0)

    def pass2(ci, state_t):
        crow = pl.ds(pl.multiple_of(ci * HGRN_SUB, HGRN_SUB), HGRN_SUB)
        oi_ref[crow, :] += _dot_nt(qf_ref[crow, :], state_t.astype(BF16))
        return state_t * dec_ref[pl.ds(ci, 1), :] + ds_ref[ci]

    lax.fori_loop(0, n_chunk, pass2, jnp.zeros((HGRN_HEAD_DIM, HGRN_HEAD_DIM), F32), unroll=4)

    gn = gn_ref[...]

    def pass3(si, carry):
        rows = slab_rows(si)
        o = oi_ref[rows, :]
        o = o * lax.rsqrt(jnp.mean(o * o, axis=-1, keepdims=True) + NORM_EPS) * gn
        g = g_ref[0, rows, :].astype(F32)
        o_ref[0, rows, :] = (o * g * (1.0 / (1.0 + jnp.exp(-g)))).astype(BF16)
        return carry

    lax.fori_loop(0, n_slab, pass3, 0, unroll=2)


def _hgrn(layer, proj, lb_logits, gn):
    bsz, seq, _ = proj.shape
    depth = lb_logits.shape[0]
    nh = HGRN_HEADS

    def col(off):
        return pl.BlockSpec((1, seq, HGRN_HEAD_DIM), lambda b, h: (b, 0, off + h))

    return pl.pallas_call(
        functools.partial(_hgrn_kernel, layer),
        grid=(bsz, nh),
        in_specs=[col(0), col(nh), col(2 * nh), col(3 * nh),
                  pl.BlockSpec((depth, HGRN_HEAD_DIM), lambda b, h: (0, h)),
                  pl.BlockSpec((1, HGRN_HEAD_DIM), lambda b, h: (0, h))],
        out_specs=pl.BlockSpec((1, seq, HGRN_HEAD_DIM), lambda b, h: (b, 0, h)),
        out_shape=jax.ShapeDtypeStruct((bsz, seq, D_HGRN), BF16),
        scratch_shapes=[
            pltpu.VMEM((seq, HGRN_HEAD_DIM), BF16),
            pltpu.VMEM((seq, HGRN_HEAD_DIM), F32),
            pltpu.VMEM((seq // HGRN_SUB, HGRN_HEAD_DIM, HGRN_HEAD_DIM), F32),
            pltpu.VMEM((seq // HGRN_SUB, HGRN_HEAD_DIM), F32),
            pltpu.VMEM((HGRN_SLAB, HGRN_HEAD_DIM), F32),
            pltpu.VMEM((HGRN_SLAB, HGRN_HEAD_DIM), F32),
            pltpu.VMEM((HGRN_SLAB, HGRN_HEAD_DIM), F32),
            pltpu.VMEM((HGRN_SLAB, HGRN_HEAD_DIM), F32),
            pltpu.SMEM((seq // HGRN_SLAB,), jnp.int32),
        ],
        compiler_params=pltpu.CompilerParams(
            dimension_semantics=("parallel", "parallel"), vmem_limit_bytes=VMEM_LIMIT),
        name="hgrn2",
    )(proj, proj, proj, proj, lb_logits, gn)


def _foxprep_kernel(fz_ref, fb_ref, fa_ref, fbm_ref):
    seq = fz_ref.shape[1]
    blk = LANES
    row = lax.broadcasted_iota(jnp.int32, (blk, blk), 0)
    col = lax.broadcasted_iota(jnp.int32, (blk, blk), 1)
    tril = jnp.where(row >= col, 1.0, 0.0).astype(BF16)
    sel = [jnp.where((col == FOX_SLOT * row + j) & (row < FOX_HEADS), 1.0, 0.0).astype(BF16)
           for j in range(6)]
    lane = lax.broadcasted_iota(jnp.int32, (1, blk), 1) % FOX_SLOT
    ones_a = jnp.where((lane >= 3) & (lane < 6), 1.0, 0.0)
    ones_b = jnp.where(lane < 3, 1.0, 0.0)
    fb = fb_ref[...]
    carry = jnp.zeros((1, blk), F32)
    for i in range(seq // blk):
        rows = slice(i * blk, (i + 1) * blk)
        log_f = _log_sigmoid(fz_ref[0, rows, :] + fb)
        csum = _dot_01_lhs(tril, log_f, 3) + carry
        carry = csum[blk - 1:blk, :]
        parts = _split_bf16(csum, 3)
        fa = ones_a
        fbm = ones_b
        for j in range(3):
            fa = fa + _dot(parts[j], sel[j])
            fbm = fbm - _dot(parts[j], sel[3 + j])
        fa_ref[0, rows, :] = fa.astype(BF16)
        fbm_ref[0, rows, :] = fbm.astype(BF16)


def _foxprep(fz, fb):
    bsz, seq, _ = fz.shape
    spec = pl.BlockSpec((1, seq, LANES), lambda b: (b, 0, 0))
    return pl.pallas_call(
        _foxprep_kernel,
        grid=(bsz,),
        in_specs=[spec, pl.BlockSpec((1, LANES), lambda b: (0, 0))],
        out_specs=[spec, spec],
        out_shape=[jax.ShapeDtypeStruct((bsz, seq, LANES), BF16)] * 2,
        compiler_params=pltpu.CompilerParams(
            dimension_semantics=("parallel",), vmem_limit_bytes=VMEM_LIMIT),
        name="foxprep",
    )(fz, fb)


def _fox_kernel(q_ref, k_ref, v_ref, g_ref, fa_ref, fbm_ref, o_ref, ka_ref, vt_ref):
    seq = q_ref.shape[1]
    tq = FOX_TQ
    pair = pl.program_id(1)
    lane = lax.broadcasted_iota(jnp.int32, (1, LANES), 1)
    key_i = lax.broadcasted_iota(jnp.int32, (tq, tq), 0)
    qry_i = lax.broadcasted_iota(jnp.int32, (tq, tq), 1)
    diag_keep = key_i <= qry_i
    neg = -1e30
    qscale = jnp.asarray(FOX_HEAD_DIM ** -0.5, BF16)
    zero = jnp.zeros((), BF16)

    ka_ref[:, :LANES] = k_ref[0]
    ka_ref[:, LANES:] = fbm_ref[0]
    vt_ref[...] = v_ref[0].astype(F32).T.astype(BF16)

    keep2 = jnp.concatenate([diag_keep, diag_keep], axis=1)
    for j in range(seq // tq):
        rows = slice(j * tq, (j + 1) * tq)
        n_off = j * tq
        q = q_ref[0, rows, :]
        fa = fa_ref[0, rows, :]
        qa = []
        for hh in range(2):
            hmask = (lane // FOX_HEAD_DIM) == hh
            fmask = (lane // FOX_SLOT) == (2 * pair + hh)
            qa.append(jnp.concatenate(
                [jnp.where(hmask, q, zero) * qscale, jnp.where(fmask, fa, zero)], axis=1))
        qa = jnp.concatenate(qa, axis=0)
        s_diag = jnp.where(keep2, _dot_nt(ka_ref[rows, :], qa), neg)
        m = jnp.max(s_diag, axis=0, keepdims=True)
        if n_off:
            s_off = _dot_nt(ka_ref[:n_off, :], qa)
            m = jnp.maximum(m, jnp.max(s_off, axis=0, keepdims=True))
        p_diag = jnp.exp(s_diag - m)
        l = jnp.sum(p_diag, axis=0, keepdims=True)
        p_diag = p_diag.astype(BF16)
        if n_off:
            p_off = jnp.exp(s_off - m)
            l = l + jnp.sum(p_off, axis=0, keepdims=True)
            p_off = p_off.astype(BF16)
        inv_l = 1.0 / l
        outs = []
        for hh in range(2):
            vt_h = slice(hh * FOX_HEAD_DIM, (hh + 1) * FOX_HEAD_DIM)
            qs = slice(hh * tq, (hh + 1) * tq)
            o_t = _dot(vt_ref[vt_h, rows], p_diag[:, qs])
            if n_off:
                o_t = o_t + _dot(vt_ref[vt_h, :n_off], p_off[:, qs])
            outs.append(o_t * inv_l[:, qs])
        o = jnp.concatenate(outs, axis=0).T
        g = g_ref[0, rows, :].astype(F32)
        o_ref[0, rows, :] = (o * g * (1.0 / (1.0 + jnp.exp(-g)))).astype(BF16)


def _fox(proj, fa, fbm):
    bsz, seq, _ = proj.shape
    npair = D_FOX // LANES
    base = 4 * D_HGRN // LANES

    def col(j):
        return pl.BlockSpec((1, seq, LANES), lambda b, h: (b, 0, base + j * npair + h))

    gate = pl.BlockSpec((1, seq, LANES), lambda b, h: (b, 0, 0))
    return pl.pallas_call(
        _fox_kernel,
        grid=(bsz, npair),
        in_specs=[col(0), col(1), col(2), col(3), gate, gate],
        out_specs=pl.BlockSpec((1, seq, LANES), lambda b, h: (b, 0, h)),
        out_shape=jax.ShapeDtypeStruct((bsz, seq, D_FOX), BF16),
        scratch_shapes=[pltpu.VMEM((seq, 2 * LANES), BF16), pltpu.VMEM((LANES, seq), BF16)],
        compiler_params=pltpu.CompilerParams(
            dimension_semantics=("parallel", "parallel"), vmem_limit_bytes=VMEM_LIMIT),
        name="fox",
    )(proj, proj, proj, proj, fa, fbm)


def _outproj_kernel(final, yh_ref, yf_ref, h_ref, p_ref, woh_ref, wof_ref, wp_ref, wg_ref,
                    fnw_ref, o_ref):
    h = h_ref[...] + _dot(yh_ref[...], woh_ref[...]) + _dot(yf_ref[...], wof_ref[...])
    ple = _dot(p_ref[...].astype(BF16), wp_ref[...])
    gate = _dot(h.astype(BF16), wg_ref[...])
    h = h + (1.0 / (1.0 + jnp.exp(-gate))) * ple
    if final:
        h = h * lax.rsqrt(jnp.mean(h * h, axis=-1, keepdims=True) + NORM_EPS) * fnw_ref[...]
    o_ref[...] = h


def _outproj(final, yh, yf, h, p, w_out_h, w_out_f, w_ple, w_gate, fnw, tm=512):
    m = h.shape[0]

    def rows(width):
        return pl.BlockSpec((tm, width), lambda i: (i, 0))

    def whole(a):
        return pl.BlockSpec(a.shape, lambda i: (0, 0))

    return pl.pallas_call(
        functools.partial(_outproj_kernel, final),
        grid=(m // tm,),
        in_specs=[rows(D_HGRN), rows(D_FOX), rows(D_MODEL), rows(PLE_DIM),
                  whole(w_out_h), whole(w_out_f), whole(w_ple), whole(w_gate), whole(fnw)],
        out_specs=rows(D_MODEL),
        out_shape=jax.ShapeDtypeStruct((m, D_MODEL), F32),
        compiler_params=pltpu.CompilerParams(
            dimension_semantics=("parallel",), vmem_limit_bytes=VMEM_LIMIT),
        name="outproj",
    )(yh, yf, h, p, w_out_h, w_out_f, w_ple, w_gate, fnw)


def kernel(x, p, norm_w, w_in, fox_fb, hgrn_gn, hgrn_lb_logits, w_out, w_ple, w_ple_gate, final_norm_w):
    bsz, seq, _ = x.shape
    depth = w_in.shape[0]
    m = bsz * seq
    w_main = w_in[:, :, :D_MAIN].astype(BF16)
    w_fgate = jnp.pad(w_in[:, :, D_MAIN:], ((0, 0), (0, 0), (0, LANES - FOX_HEADS))).astype(BF16)
    fb = jnp.pad(fox_fb, ((0, 0), (0, LANES - FOX_HEADS)))
    w_out_b = w_out.astype(BF16)
    w_ple_b = w_ple.astype(BF16)
    w_gate_b = w_ple_gate.astype(BF16)
    fnw = final_norm_w.reshape(1, D_MODEL)

    h = x.reshape(m, D_MODEL)
    for i in range(depth):
        proj, fz = _inproj(h, norm_w[i].reshape(1, D_MODEL), w_main[i], w_fgate[i])
        proj = proj.reshape(bsz, seq, D_MAIN)
        yh = _hgrn(i, proj, hgrn_lb_logits, hgrn_gn[i].reshape(1, D_HGRN))
        fa, fbm = _foxprep(fz.reshape(bsz, seq, LANES), fb[i].reshape(1, LANES))
        yf = _fox(proj, fa, fbm)
        h = _outproj(i == depth - 1, yh.reshape(m, D_HGRN), yf.reshape(m, D_FOX), h,
                     p[i].reshape(m, PLE_DIM), w_out_b[i, :D_HGRN], w_out_b[i, D_HGRN:],
                     w_ple_b[i], w_gate_b[i], fnw)
    return h.reshape(bsz, seq, D_MODEL)
```

```python
import functools

import jax
import jax.numpy as jnp
from jax import lax
from jax.experimental import pallas as pl
from jax.experimental.pallas import tpu as pltpu

D_MODEL = 1024
PLE_DIM = 256
D_HGRN = 1024
D_FOX = 1024
HGRN_HEAD_DIM = 128
HGRN_HEADS = D_HGRN // HGRN_HEAD_DIM
FOX_HEAD_DIM = 64
FOX_HEADS = D_FOX // FOX_HEAD_DIM
NORM_EPS = 1e-6
D_MAIN = 4 * D_HGRN + 4 * D_FOX

LANES = 128
HGRN_SUB = 32
HGRN_SLAB = 128
FOX_TQ = 256
FOX_SLOT = 8
HGRN_FAST_MIN_LOG_DECAY = -60.0
VMEM_LIMIT = 48 * 1024 * 1024

F32 = jnp.float32
BF16 = jnp.bfloat16


def _dot(a, b):
    return jnp.dot(a, b, preferred_element_type=F32)


def _dot_nt(a, b):
    return lax.dot_general(a, b, (((1,), (1,)), ((), ())), preferred_element_type=F32)


def _split_bf16(x, n):
    parts = []
    r = x
    for _ in range(n - 1):
        hi = r.astype(BF16)
        parts.append(hi)
        r = r - hi.astype(F32)
    parts.append(r.astype(BF16))
    return parts


def _dot_01_lhs(m01, x, n):
    acc = None
    for part in _split_bf16(x, n):
        t = _dot(m01, part)
        acc = t if acc is None else acc + t
    return acc


def _log_sigmoid(z):
    return jnp.minimum(z, 0.0) - jnp.log1p(jnp.exp(-jnp.abs(z)))


def _inproj_kernel(x_ref, nw_ref, w_ref, wz_ref, proj_ref, fz_ref, u_ref):
    @pl.when(pl.program_id(1) == 0)
    def _():
        x = x_ref[...]
        ms = jnp.mean(x * x, axis=-1, keepdims=True)
        u = (x * lax.rsqrt(ms + NORM_EPS) * nw_ref[...]).astype(BF16)
        u_ref[...] = u
        fz_ref[...] = _dot(u, wz_ref[...])

    proj_ref[...] = _dot(u_ref[...], w_ref[...]).astype(BF16)


def _inproj(h, nw, w_main, w_gate, tm=1024, tn=2048):
    m = h.shape[0]
    return pl.pallas_call(
        _inproj_kernel,
        grid=(m // tm, D_MAIN // tn),
        in_specs=[
            pl.BlockSpec((tm, D_MODEL), lambda i, j: (i, 0)),
            pl.BlockSpec((1, D_MODEL), lambda i, j: (0, 0)),
            pl.BlockSpec((D_MODEL, tn), lambda i, j: (0, j)),
            pl.BlockSpec((D_MODEL, LANES), lambda i, j: (0, 0)),
        ],
        out_specs=[
            pl.BlockSpec((tm, tn), lambda i, j: (i, j)),
            pl.BlockSpec((tm, LANES), lambda i, j: (i, 0)),
        ],
        out_shape=[
            jax.ShapeDtypeStruct((m, D_MAIN), BF16),
            jax.ShapeDtypeStruct((m, LANES), F32),
        ],
        scratch_shapes=[pltpu.VMEM((tm, D_MODEL), BF16)],
        compiler_params=pltpu.CompilerParams(
            dimension_semantics=("parallel", "arbitrary"), vmem_limit_bytes=VMEM_LIMIT),
        name="inproj",
    )(h, nw, w_main, w_gate)


def _hgrn_kernel(layer, q_ref, f_ref, i_ref, g_ref, lbl_ref, gn_ref, o_ref,
                 qf_ref, oi_ref, ds_ref, dec_ref, qs_ref, ks_ref, bs_ref, vs_ref, flag_ref):
    seq = q_ref.shape[1]
    n_slab = seq // HGRN_SLAB
    n_sub = HGRN_SLAB // HGRN_SUB
    n_chunk = seq // HGRN_SUB

    if layer > 0:
        logits = lbl_ref[...]
        ex = jnp.exp(logits - jnp.max(logits, axis=0, keepdims=True))
        sm = ex / jnp.sum(ex, axis=0, keepdims=True)
        lb = jnp.sum(sm[1:layer + 1], axis=0, keepdims=True)
        log_lb = jnp.log(lb)
        log_1m_lb = jnp.log1p(-lb)
        one_m_lb = 1.0 - lb

    row = lax.broadcasted_iota(jnp.int32, (HGRN_SLAB, HGRN_SLAB), 0)
    col = lax.broadcasted_iota(jnp.int32, (HGRN_SLAB, HGRN_SLAB), 1)
    causal = (row >= col) & ((row // HGRN_SUB) == (col // HGRN_SUB))
    tril = jnp.where(causal, 1.0, 0.0).astype(BF16)
    row_chunk = lax.broadcasted_iota(jnp.int32, (HGRN_SLAB, HGRN_HEAD_DIM), 0) // HGRN_SUB
    scale = HGRN_HEAD_DIM ** -0.5

    def slab_rows(si):
        return pl.ds(pl.multiple_of(si * HGRN_SLAB, HGRN_SLAB), HGRN_SLAB)

    def decays(rows):
        fz = f_ref[0, rows, :].astype(F32)
        e = jnp.exp(-jnp.abs(fz))
        log_sig = jnp.minimum(fz, 0.0) - jnp.log1p(e)
        r = 1.0 / (1.0 + e)
        sig_neg = jnp.where(fz >= 0.0, e * r, r)
        if layer > 0:
            bb = log_1m_lb + log_sig
            log_f = jnp.maximum(log_lb, bb) + jnp.log1p(jnp.exp(-jnp.abs(log_lb - bb)))
            k = one_m_lb * sig_neg
        else:
            log_f = log_sig
            k = sig_neg
        hi, lo = _split_bf16(log_f, 2)
        csum = _dot(tril, jnp.concatenate([hi, lo], axis=1))
        return k, csum[:, :HGRN_HEAD_DIM] + csum[:, HGRN_HEAD_DIM:]

    def pass1(si, carry):
        rows = slab_rows(si)
        k, b = decays(rows)
        b_last = jnp.concatenate(
            [jnp.broadcast_to(b[(c + 1) * HGRN_SUB - 1:(c + 1) * HGRN_SUB, :], (HGRN_SUB, HGRN_HEAD_DIM))
             for c in range(n_sub)], axis=0)
        q = q_ref[0, rows, :].astype(F32) * scale
        v = i_ref[0, rows, :]
        qf = (q * jnp.exp(b)).astype(BF16)
        kf = (k * jnp.exp(-b)).astype(BF16)
        kb = (k * jnp.exp(b_last - b)).astype(BF16)
        scores = jnp.where(causal, _dot_nt(qf, kf), 0.0)
        oi_ref[rows, :] = _dot(scores.astype(BF16), v)
        qf_ref[rows, :] = qf
        v_t = v.astype(F32).T.astype(BF16)
        zero = jnp.zeros((), BF16)
        for c in range(0, n_sub, 2):
            kb2 = jnp.concatenate([jnp.where(row_chunk == c, kb, zero),
                                   jnp.where(row_chunk == c + 1, kb, zero)], axis=1)
            ds2 = _dot(v_t, kb2)
            ds_ref[si * n_sub + c] = ds2[:, :HGRN_HEAD_DIM]
            ds_ref[si * n_sub + c + 1] = ds2[:, HGRN_HEAD_DIM:]
        for c in range(n_sub):
            ci = si * n_sub + c
            dec_ref[pl.ds(ci, 1), :] = jnp.exp(b[(c + 1) * HGRN_SUB - 1:(c + 1) * HGRN_SUB, :])
        flag_ref[si] = (jnp.min(b) < HGRN_FAST_MIN_LOG_DECAY).astype(jnp.int32)
        return carry

    lax.fori_loop(0, n_slab, pass1, 0, unroll=16)

    def redo_direct(si, carry):
        @pl.when(flag_ref[si] != 0)
        def _():
            rows = slab_rows(si)
            r0 = pl.multiple_of(si * HGRN_SLAB, HGRN_SLAB)
            k, b = decays(rows)
            qs_ref[...] = q_ref[0, rows, :].astype(F32) * scale
            ks_ref[...] = k
            bs_ref[...] = b
            vs_ref[...] = i_ref[0, rows, :].astype(F32)
            sub_row = lax.broadcasted_iota(jnp.int32, (HGRN_SUB, 1), 0)

            def one_row(t, c):
                c0 = pl.multiple_of((t // HGRN_SUB) * HGRN_SUB, HGRN_SUB)
                crow = pl.ds(c0, HGRN_SUB)
                d = bs_ref[pl.ds(t, 1), :] - bs_ref[crow, :]
                keep = (sub_row + c0) <= t
                w = jnp.where(keep, jnp.exp(jnp.minimum(d, 0.0)), 0.0)
                sc = jnp.sum(qs_ref[pl.ds(t, 1), :] * ks_ref[crow, :] * w, axis=1, keepdims=True)
                oi_ref[pl.ds(r0 + t, 1), :] = jnp.sum(sc * vs_ref[crow, :], axis=0, keepdims=True)
                return c

            lax.fori_loop(0, HGRN_SLAB, one_row, 0)
        return carry

    lax.fori_loop(0, n_slab, redo_direct, 0)

    gn = gn_ref[...]

    def pass2(ci, state_t):
        crow = pl.ds(pl.multiple_of(ci * HGRN_SUB, HGRN_SUB), HGRN_SUB)
        o = oi_ref[crow, :] + _dot_nt(qf_ref[crow, :], state_t.astype(BF16))
        o = o * lax.rsqrt(jnp.mean(o * o, axis=-1, keepdims=True) + NORM_EPS) * gn
        g = g_ref[0, crow, :].astype(F32)
        o_ref[0, crow, :] = (o * g * (1.0 / (1.0 + jnp.exp(-g)))).astype(BF16)
        return state_t * dec_ref[pl.ds(ci, 1), :] + ds_ref[ci]

    lax.fori_loop(0, n_chunk, pass2, jnp.zeros((HGRN_HEAD_DIM, HGRN_HEAD_DIM), F32), unroll=32)


def _hgrn(layer, proj, lb_logits, gn):
    bsz, seq, _ = proj.shape
    depth = lb_logits.shape[0]
    nh = HGRN_HEADS

    def col(off):
        return pl.BlockSpec((1, seq, HGRN_HEAD_DIM), lambda b, h: (b, 0, off + h))

    return pl.pallas_call(
        functools.partial(_hgrn_kernel, layer),
        grid=(bsz, nh),
        in_specs=[col(0), col(nh), col(2 * nh), col(3 * nh),
                  pl.BlockSpec((depth, HGRN_HEAD_DIM), lambda b, h: (0, h)),
                  pl.BlockSpec((1, HGRN_HEAD_DIM), lambda b, h: (0, h))],
        out_specs=pl.BlockSpec((1, seq, HGRN_HEAD_DIM), lambda b, h: (b, 0, h)),
        out_shape=jax.ShapeDtypeStruct((bsz, seq, D_HGRN), BF16),
        scratch_shapes=[
            pltpu.VMEM((seq, HGRN_HEAD_DIM), BF16),
            pltpu.VMEM((seq, HGRN_HEAD_DIM), F32),
            pltpu.VMEM((seq // HGRN_SUB, HGRN_HEAD_DIM, HGRN_HEAD_DIM), F32),
            pltpu.VMEM((seq // HGRN_SUB, HGRN_HEAD_DIM), F32),
            pltpu.VMEM((HGRN_SLAB, HGRN_HEAD_DIM), F32),
            pltpu.VMEM((HGRN_SLAB, HGRN_HEAD_DIM), F32),
            pltpu.VMEM((HGRN_SLAB, HGRN_HEAD_DIM), F32),
            pltpu.VMEM((HGRN_SLAB, HGRN_HEAD_DIM), F32),
            pltpu.SMEM((seq // HGRN_SLAB,), jnp.int32),
        ],
        compiler_params=pltpu.CompilerParams(
            dimension_semantics=("parallel", "parallel"), vmem_limit_bytes=VMEM_LIMIT),
        name="hgrn2",
    )(proj, proj, proj, proj, lb_logits, gn)


def _foxprep_kernel(fz_ref, fb_ref, fa_ref, fbm_ref):
    seq = fz_ref.shape[1]
    blk = LANES
    row = lax.broadcasted_iota(jnp.int32, (blk, blk), 0)
    col = lax.broadcasted_iota(jnp.int32, (blk, blk), 1)
    tril = jnp.where(row >= col, 1.0, 0.0).astype(BF16)
    sel = [jnp.where((col == FOX_SLOT * row + j) & (row < FOX_HEADS), 1.0, 0.0).astype(BF16)
           for j in range(6)]
    lane = lax.broadcasted_iota(jnp.int32, (1, blk), 1) % FOX_SLOT
    ones_a = jnp.where((lane >= 3) & (lane < 6), 1.0, 0.0)
    ones_b = jnp.where(lane < 3, 1.0, 0.0)
    fb = fb_ref[...]
    carry = jnp.zeros((1, blk), F32)
    for i in range(seq // blk):
        rows = slice(i * blk, (i + 1) * blk)
        log_f = _log_sigmoid(fz_ref[0, rows, :] + fb)
        csum = _dot_01_lhs(tril, log_f, 3) + carry
        carry = csum[blk - 1:blk, :]
        parts = _split_bf16(csum, 3)
        fa = ones_a
        fbm = ones_b
        for j in range(3):
            fa = fa + _dot(parts[j], sel[j])
            fbm = fbm - _dot(parts[j], sel[3 + j])
        fa_ref[0, rows, :] = fa.astype(BF16)
        fbm_ref[0, rows, :] = fbm.astype(BF16)


def _foxprep(fz, fb):
    bsz, seq, _ = fz.shape
    spec = pl.BlockSpec((1, seq, LANES), lambda b: (b, 0, 0))
    return pl.pallas_call(
        _foxprep_kernel,
        grid=(bsz,),
        in_specs=[spec, pl.BlockSpec((1, LANES), lambda b: (0, 0))],
        out_specs=[spec, spec],
        out_shape=[jax.ShapeDtypeStruct((bsz, seq, LANES), BF16)] * 2,
        compiler_params=pltpu.CompilerParams(
            dimension_semantics=("parallel",), vmem_limit_bytes=VMEM_LIMIT),
        name="foxprep",
    )(fz, fb)


def _fox_kernel(q_ref, k_ref, v_ref, g_ref, fa_ref, fbm_ref, o_ref, ka_ref, vt_ref):
    seq = q_ref.shape[1]
    tq = FOX_TQ
    pair = pl.program_id(1)
    lane = lax.broadcasted_iota(jnp.int32, (1, LANES), 1)
    key_i = lax.broadcasted_iota(jnp.int32, (tq, tq), 0)
    qry_i = lax.broadcasted_iota(jnp.int32, (tq, tq), 1)
    diag_keep = key_i <= qry_i
    neg = -1e30
    qscale = jnp.asarray(FOX_HEAD_DIM ** -0.5, BF16)
    zero = jnp.zeros((), BF16)

    ka_ref[:, :LANES] = k_ref[0]
    ka_ref[:, LANES:] = fbm_ref[0]
    vt_ref[...] = v_ref[0].astype(F32).T.astype(BF16)

    keep2 = jnp.concatenate([diag_keep, diag_keep], axis=1)
    for j in range(seq // tq):
        rows = slice(j * tq, (j + 1) * tq)
        n_off = j * tq
        q = q_ref[0, rows, :]
        fa = fa_ref[0, rows, :]
        qa = []
        for hh in range(2):
            hmask = (lane // FOX_HEAD_DIM) == hh
            fmask = (lane // FOX_SLOT) == (2 * pair + hh)
            qa.append(jnp.concatenate(
                [jnp.where(hmask, q, zero) * qscale, jnp.where(fmask, fa, zero)], axis=1))
        qa = jnp.concatenate(qa, axis=0)
        s_diag = jnp.where(keep2, _dot_nt(ka_ref[rows, :], qa), neg)
        m = jnp.max(s_diag, axis=0, keepdims=True)
        if n_off:
            s_off = _dot_nt(ka_ref[:n_off, :], qa)
            m = jnp.maximum(m, jnp.max(s_off, axis=0, keepdims=True))
        p_diag = jnp.exp(s_diag - m)
        l = jnp.sum(p_diag, axis=0, keepdims=True)
        p_diag = p_diag.astype(BF16)
        if n_off:
            p_off = jnp.exp(s_off - m)
            l = l + jnp.sum(p_off, axis=0, keepdims=True)
            p_off = p_off.astype(BF16)
        inv_l = 1.0 / l
        outs = []
        for hh in range(2):
            vt_h = slice(hh * FOX_HEAD_DIM, (hh + 1) * FOX_HEAD_DIM)
            qs = slice(hh * tq, (hh + 1) * tq)
            o_t = _dot(vt_ref[vt_h, rows], p_diag[:, qs])
            if n_off:
                o_t = o_t + _dot(vt_ref[vt_h, :n_off], p_off[:, qs])
            outs.append(o_t * inv_l[:, qs])
        o = jnp.concatenate(outs, axis=0).T
        g = g_ref[0, rows, :].astype(F32)
        o_ref[0, rows, :] = (o * g * (1.0 / (1.0 + jnp.exp(-g)))).astype(BF16)


def _fox(proj, fa, fbm):
    bsz, seq, _ = proj.shape
    npair = D_FOX // LANES
    base = 4 * D_HGRN // LANES

    def col(j):
        return pl.BlockSpec((1, seq, LANES), lambda b, h: (b, 0, base + j * npair + h))

    gate = pl.BlockSpec((1, seq, LANES), lambda b, h: (b, 0, 0))
    return pl.pallas_call(
        _fox_kernel,
        grid=(bsz, npair),
        in_specs=[col(0), col(1), col(2), col(3), gate, gate],
        out_specs=pl.BlockSpec((1, seq, LANES), lambda b, h: (b, 0, h)),
        out_shape=jax.ShapeDtypeStruct((bsz, seq, D_FOX), BF16),
        scratch_shapes=[pltpu.VMEM((seq, 2 * LANES), BF16), pltpu.VMEM((LANES, seq), BF16)],
        compiler_params=pltpu.CompilerParams(
            dimension_semantics=("parallel", "parallel"), vmem_limit_bytes=VMEM_LIMIT),
        name="fox",
    )(proj, proj, proj, proj, fa, fbm)


def _outproj_kernel(final, yh_ref, yf_ref, h_ref, p_ref, woh_ref, wof_ref, wp_ref, wg_ref,
                    fnw_ref, o_ref):
    h = h_ref[...] + _dot(yh_ref[...], woh_ref[...]) + _dot(yf_ref[...], wof_ref[...])
    ple = _dot(p_ref[...].astype(BF16), wp_ref[...])
    gate = _dot(h.astype(BF16), wg_ref[...])
    h = h + (1.0 / (1.0 + jnp.exp(-gate))) * ple
    if final:
        h = h * lax.rsqrt(jnp.mean(h * h, axis=-1, keepdims=True) + NORM_EPS) * fnw_ref[...]
    o_ref[...] = h


def _outproj(final, yh, yf, h, p, w_out_h, w_out_f, w_ple, w_gate, fnw, tm=512):
    m = h.shape[0]

    def rows(width):
        return pl.BlockSpec((tm, width), lambda i: (i, 0))

    def whole(a):
        return pl.BlockSpec(a.shape, lambda i: (0, 0))

    return pl.pallas_call(
        functools.partial(_outproj_kernel, final),
        grid=(m // tm,),
        in_specs=[rows(D_HGRN), rows(D_FOX), rows(D_MODEL), rows(PLE_DIM),
                  whole(w_out_h), whole(w_out_f), whole(w_ple), whole(w_gate), whole(fnw)],
        out_specs=rows(D_MODEL),
        out_shape=jax.ShapeDtypeStruct((m, D_MODEL), F32),
        compiler_params=pltpu.CompilerParams(
            dimension_semantics=("parallel",), vmem_limit_bytes=VMEM_LIMIT),
        name="outproj",
    )(yh, yf, h, p, w_out_h, w_out_f, w_ple, w_gate, fnw)


def kernel(x, p, norm_w, w_in, fox_fb, hgrn_gn, hgrn_lb_logits, w_out, w_ple, w_ple_gate, final_norm_w):
    bsz, seq, _ = x.shape
    depth = w_in.shape[0]
    m = bsz * seq
    w_main = w_in[:, :, :D_MAIN].astype(BF16)
    w_fgate = jnp.pad(w_in[:, :, D_MAIN:], ((0, 0), (0, 0), (0, LANES - FOX_HEADS))).astype(BF16)
    fb = jnp.pad(fox_fb, ((0, 0), (0, LANES - FOX_HEADS)))
    w_out_b = w_out.astype(BF16)
    w_ple_b = w_ple.astype(BF16)
    w_gate_b = w_ple_gate.astype(BF16)
    fnw = final_norm_w.reshape(1, D_MODEL)

    h = x.reshape(m, D_MODEL)
    for i in range(depth):
        proj, fz = _inproj(h, norm_w[i].reshape(1, D_MODEL), w_main[i], w_fgate[i])
        proj = proj.reshape(bsz, seq, D_MAIN)
        yh = _hgrn(i, proj, hgrn_lb_logits, hgrn_gn[i].reshape(1, D_HGRN))
        fa, fbm = _foxprep(fz.reshape(bsz, seq, LANES), fb[i].reshape(1, LANES))
        yf = _fox(proj, fa, fbm)
        h = _outproj(i == depth - 1, yh.reshape(m, D_HGRN), yf.reshape(m, D_FOX), h,
                     p[i].reshape(m, PLE_DIM), w_out_b[i, :D_HGRN], w_out_b[i, D_HGRN:],
                     w_ple_b[i], w_gate_b[i], fnw)
    return h.reshape(bsz, seq, D_MODEL)
```

```python
import functools

import jax
import jax.numpy as jnp
from jax import lax
from jax.experimental import pallas as pl
from jax.experimental.pallas import tpu as pltpu

D_MODEL = 1024
PLE_DIM = 256
D_HGRN = 1024
D_FOX = 1024
HGRN_HEAD_DIM = 128
HGRN_HEADS = D_HGRN // HGRN_HEAD_DIM
FOX_HEAD_DIM = 64
FOX_HEADS = D_FOX // FOX_HEAD_DIM
NORM_EPS = 1e-6
D_MAIN = 4 * D_HGRN + 4 * D_FOX

LANES = 128
HGRN_SUB = 32
HGRN_SLAB = 128
FOX_TQ = 256
FOX_SLOT = 8
FOX_ONES_ROWS = 16
LOG2E = 1.4426950408889634
HGRN_FAST_MIN_LOG_DECAY = -60.0
VMEM_LIMIT = 48 * 1024 * 1024

F32 = jnp.float32
BF16 = jnp.bfloat16


def _dot(a, b):
    return jnp.dot(a, b, preferred_element_type=F32)


def _dot_nt(a, b):
    return lax.dot_general(a, b, (((1,), (1,)), ((), ())), preferred_element_type=F32)


def _split_bf16(x, n):
    parts = []
    r = x
    for _ in range(n - 1):
        hi = r.astype(BF16)
        parts.append(hi)
        r = r - hi.astype(F32)
    parts.append(r.astype(BF16))
    return parts


def _dot_01_lhs(m01, x, n):
    acc = None
    for part in _split_bf16(x, n):
        t = _dot(m01, part)
        acc = t if acc is None else acc + t
    return acc


def _log_sigmoid(z):
    return jnp.minimum(z, 0.0) - jnp.log1p(jnp.exp(-jnp.abs(z)))


def _inproj_kernel(x_ref, nw_ref, w_ref, wz_ref, proj_ref, fz_ref, u_ref):
    @pl.when(pl.program_id(1) == 0)
    def _():
        x = x_ref[...]
        ms = jnp.mean(x * x, axis=-1, keepdims=True)
        u = (x * lax.rsqrt(ms + NORM_EPS) * nw_ref[...]).astype(BF16)
        u_ref[...] = u
        fz_ref[...] = _dot(u, wz_ref[...])

    proj_ref[...] = _dot(u_ref[...], w_ref[...]).astype(BF16)


def _inproj(h, nw, w_main, w_gate, tm=1024, tn=2048):
    m = h.shape[0]
    return pl.pallas_call(
        _inproj_kernel,
        grid=(m // tm, D_MAIN // tn),
        in_specs=[
            pl.BlockSpec((tm, D_MODEL), lambda i, j: (i, 0)),
            pl.BlockSpec((1, D_MODEL), lambda i, j: (0, 0)),
            pl.BlockSpec((D_MODEL, tn), lambda i, j: (0, j)),
            pl.BlockSpec((D_MODEL, LANES), lambda i, j: (0, 0)),
        ],
        out_specs=[
            pl.BlockSpec((tm, tn), lambda i, j: (i, j)),
            pl.BlockSpec((tm, LANES), lambda i, j: (i, 0)),
        ],
        out_shape=[
            jax.ShapeDtypeStruct((m, D_MAIN), BF16),
            jax.ShapeDtypeStruct((m, LANES), F32),
        ],
        scratch_shapes=[pltpu.VMEM((tm, D_MODEL), BF16)],
        compiler_params=pltpu.CompilerParams(
            dimension_semantics=("parallel", "arbitrary"), vmem_limit_bytes=VMEM_LIMIT),
        name="inproj",
    )(h, nw, w_main, w_gate)


def _hgrn_kernel(layer, q_ref, f_ref, i_ref, g_ref, lbl_ref, gn_ref, o_ref,
                 qf_ref, oi_ref, ds_ref, dec_ref, qs_ref, ks_ref, bs_ref, vs_ref, flag_ref):
    seq = q_ref.shape[1]
    n_slab = seq // HGRN_SLAB
    n_sub = HGRN_SLAB // HGRN_SUB
    n_chunk = seq // HGRN_SUB

    if layer > 0:
        logits = lbl_ref[...]
        ex = jnp.exp(logits - jnp.max(logits, axis=0, keepdims=True))
        sm = ex / jnp.sum(ex, axis=0, keepdims=True)
        lb = jnp.sum(sm[1:layer + 1], axis=0, keepdims=True)
        log_lb = jnp.log(lb)
        log_1m_lb = jnp.log1p(-lb)
        one_m_lb = 1.0 - lb

    row = lax.broadcasted_iota(jnp.int32, (HGRN_SLAB, HGRN_SLAB), 0)
    col = lax.broadcasted_iota(jnp.int32, (HGRN_SLAB, HGRN_SLAB), 1)
    causal = (row >= col) & ((row // HGRN_SUB) == (col // HGRN_SUB))
    tril = jnp.where(causal, 1.0, 0.0).astype(BF16)
    row_chunk = lax.broadcasted_iota(jnp.int32, (HGRN_SLAB, HGRN_HEAD_DIM), 0) // HGRN_SUB
    scale = HGRN_HEAD_DIM ** -0.5
    gn = gn_ref[...]

    def slab_rows(si):
        return pl.ds(pl.multiple_of(si * HGRN_SLAB, HGRN_SLAB), HGRN_SLAB)

    def decays(rows):
        fz = f_ref[0, rows, :].astype(F32)
        e = jnp.exp(-jnp.abs(fz))
        log_sig = jnp.minimum(fz, 0.0) - jnp.log1p(e)
        r = 1.0 / (1.0 + e)
        sig_neg = jnp.where(fz >= 0.0, e * r, r)
        if layer > 0:
            bb = log_1m_lb + log_sig
            log_f = jnp.maximum(log_lb, bb) + jnp.log1p(jnp.exp(-jnp.abs(log_lb - bb)))
            k = one_m_lb * sig_neg
        else:
            log_f = log_sig
            k = sig_neg
        hi, lo = _split_bf16(log_f, 2)
        csum = _dot(tril, jnp.concatenate([hi, lo], axis=1))
        return k, csum[:, :HGRN_HEAD_DIM] + csum[:, HGRN_HEAD_DIM:]

    def pass1_a(si):
        return decays(pl.ds(si * HGRN_SLAB, HGRN_SLAB))

    def pass1_b(si, k, b):
        rows = pl.ds(si * HGRN_SLAB, HGRN_SLAB)
        b_last = jnp.concatenate(
            [jnp.broadcast_to(b[(c + 1) * HGRN_SUB - 1:(c + 1) * HGRN_SUB, :], (HGRN_SUB, HGRN_HEAD_DIM))
             for c in range(n_sub)], axis=0)
        q = q_ref[0, rows, :].astype(F32) * scale
        qf = (q * jnp.exp(b)).astype(BF16)
        kf = (k * jnp.exp(-b)).astype(BF16)
        kb = (k * jnp.exp(b_last - b)).astype(BF16)
        qf_ref[rows, :] = qf
        return b, kb, _dot_nt(qf, kf)

    def pass1_c(si, b, kb, scores):
        rows = pl.ds(si * HGRN_SLAB, HGRN_SLAB)
        v = i_ref[0, rows, :]
        oi_ref[rows, :] = _dot(jnp.where(causal, scores, 0.0).astype(BF16), v)
        v_t = v.astype(F32).T.astype(BF16)
        zero = jnp.zeros((), BF16)
        for c in range(0, n_sub, 2):
            kb2 = jnp.concatenate([jnp.where(row_chunk == c, kb, zero),
                                   jnp.where(row_chunk == c + 1, kb, zero)], axis=1)
            ds2 = _dot(v_t, kb2)
            ds_ref[si * n_sub + c] = ds2[:, :HGRN_HEAD_DIM]
            ds_ref[si * n_sub + c + 1] = ds2[:, HGRN_HEAD_DIM:]
        for c in range(n_sub):
            ci = si * n_sub + c
            dec_ref[pl.ds(ci, 1), :] = jnp.exp(b[(c + 1) * HGRN_SUB - 1:(c + 1) * HGRN_SUB, :])
        flag_ref[si] = (jnp.min(b) < HGRN_FAST_MIN_LOG_DECAY).astype(jnp.int32)

    def redo_slab(si, carry):
        @pl.when(flag_ref[si] != 0)
        def _():
            rows = slab_rows(si)
            r0 = pl.multiple_of(si * HGRN_SLAB, HGRN_SLAB)
            k, b = decays(rows)
            qs_ref[...] = q_ref[0, rows, :].astype(F32) * scale
            ks_ref[...] = k
            bs_ref[...] = b
            vs_ref[...] = i_ref[0, rows, :].astype(F32)
            sub_row = lax.broadcasted_iota(jnp.int32, (HGRN_SUB, 1), 0)

            def one_row(t, c):
                c0 = pl.multiple_of((t // HGRN_SUB) * HGRN_SUB, HGRN_SUB)
                crow = pl.ds(c0, HGRN_SUB)
                d = bs_ref[pl.ds(t, 1), :] - bs_ref[crow, :]
                keep = (sub_row + c0) <= t
                w = jnp.where(keep, jnp.exp(jnp.minimum(d, 0.0)), 0.0)
                sc = jnp.sum(qs_ref[pl.ds(t, 1), :] * ks_ref[crow, :] * w, axis=1, keepdims=True)
                oi_ref[pl.ds(r0 + t, 1), :] = jnp.sum(sc * vs_ref[crow, :], axis=0, keepdims=True)
                return c

            lax.fori_loop(0, HGRN_SLAB, one_row, 0)
        return carry

    def redo():
        lax.fori_loop(0, n_slab, redo_slab, 0)

    def pass2(ci, state_t):
        crow = pl.ds(ci * HGRN_SUB, HGRN_SUB)
        o = oi_ref[crow, :] + _dot_nt(qf_ref[crow, :], state_t.astype(BF16))
        o = o * lax.rsqrt(jnp.mean(o * o, axis=-1, keepdims=True) + NORM_EPS) * gn
        g = g_ref[0, crow, :].astype(F32)
        o_ref[0, crow, :] = (o * g * (1.0 / (1.0 + jnp.exp(-g)))).astype(BF16)
        return state_t * dec_ref[pl.ds(ci, 1), :] + ds_ref[ci]

    stage_a = {0: pass1_a(0)}
    stage_b = {}
    for step in range(n_slab + 1):
        if step + 1 < n_slab:
            stage_a[step + 1] = pass1_a(step + 1)
        if step < n_slab:
            stage_b[step] = pass1_b(step, *stage_a.pop(step))
        if step >= 1:
            pass1_c(step - 1, *stage_b.pop(step - 1))
    redo()
    state_t = jnp.zeros((HGRN_HEAD_DIM, HGRN_HEAD_DIM), F32)
    for ci in range(n_chunk):
        state_t = pass2(ci, state_t)


def _hgrn(layer, proj, lb_logits, gn):
    bsz, seq, _ = proj.shape
    depth = lb_logits.shape[0]
    nh = HGRN_HEADS

    def col(j):
        return pl.BlockSpec((1, seq, HGRN_HEAD_DIM), lambda b, h: (b, 0, j * nh + h))

    slab = pltpu.VMEM((HGRN_SLAB, HGRN_HEAD_DIM), F32)
    return pl.pallas_call(
        functools.partial(_hgrn_kernel, layer),
        grid=(bsz, nh),
        in_specs=[col(0), col(1), col(2), col(3),
                  pl.BlockSpec((depth, HGRN_HEAD_DIM), lambda b, h: (0, h)),
                  pl.BlockSpec((1, HGRN_HEAD_DIM), lambda b, h: (0, h))],
        out_specs=pl.BlockSpec((1, seq, HGRN_HEAD_DIM), lambda b, h: (b, 0, h)),
        out_shape=jax.ShapeDtypeStruct((bsz, seq, D_HGRN), BF16),
        scratch_shapes=[
            pltpu.VMEM((seq, HGRN_HEAD_DIM), BF16),
            pltpu.VMEM((seq, HGRN_HEAD_DIM), F32),
            pltpu.VMEM((seq // HGRN_SUB, HGRN_HEAD_DIM, HGRN_HEAD_DIM), F32),
            pltpu.VMEM((seq // HGRN_SUB, HGRN_HEAD_DIM), F32),
            slab, slab, slab, slab,
            pltpu.SMEM((seq // HGRN_SLAB,), jnp.int32),
        ],
        compiler_params=pltpu.CompilerParams(
            dimension_semantics=("parallel", "parallel"), vmem_limit_bytes=VMEM_LIMIT),
        name="hgrn2",
    )(proj, proj, proj, proj, lb_logits, gn)


def _foxprep_kernel(fz_ref, fb_ref, fa_ref, fbm_ref):
    seq = fz_ref.shape[1]
    blk = LANES
    row = lax.broadcasted_iota(jnp.int32, (blk, blk), 0)
    col = lax.broadcasted_iota(jnp.int32, (blk, blk), 1)
    tril = jnp.where(row >= col, 1.0, 0.0).astype(BF16)
    sel = [jnp.where((col == FOX_SLOT * row + j) & (row < FOX_HEADS), 1.0, 0.0).astype(BF16)
           for j in range(6)]
    lane = lax.broadcasted_iota(jnp.int32, (1, blk), 1) % FOX_SLOT
    ones_a = jnp.where((lane >= 3) & (lane < 6), 1.0, 0.0)
    ones_b = jnp.where(lane < 3, 1.0, 0.0)
    fb = fb_ref[...]
    carry = jnp.zeros((1, blk), F32)
    for i in range(seq // blk):
        rows = slice(i * blk, (i + 1) * blk)
        log_f = _log_sigmoid(fz_ref[0, rows, :] + fb)
        csum = _dot_01_lhs(tril, log_f, 3) + carry
        carry = csum[blk - 1:blk, :]
        parts = _split_bf16(csum * LOG2E, 3)
        fa = ones_a
        fbm = ones_b
        for j in range(3):
            fa = fa + _dot(parts[j], sel[j])
            fbm = fbm - _dot(parts[j], sel[3 + j])
        fa_ref[0, rows, :] = fa.astype(BF16)
        fbm_ref[0, rows, :] = fbm.astype(BF16)


def _foxprep(fz, fb):
    bsz, seq, _ = fz.shape
    spec = pl.BlockSpec((1, seq, LANES), lambda b: (b, 0, 0))
    return pl.pallas_call(
        _foxprep_kernel,
        grid=(bsz,),
        in_specs=[spec, pl.BlockSpec((1, LANES), lambda b: (0, 0))],
        out_specs=[spec, spec],
        out_shape=[jax.ShapeDtypeStruct((bsz, seq, LANES), BF16)] * 2,
        compiler_params=pltpu.CompilerParams(
            dimension_semantics=("parallel",), vmem_limit_bytes=VMEM_LIMIT),
        name="foxprep",
    )(fz, fb)


def _fox_kernel(q_ref, k_ref, v_ref, g_ref, fa_ref, fbm_ref, o_ref, ka_ref, vt_ref):
    seq = q_ref.shape[1]
    tq = FOX_TQ
    pair = pl.program_id(1)
    lane = lax.broadcasted_iota(jnp.int32, (1, LANES), 1)
    key_i = lax.broadcasted_iota(jnp.int32, (tq, tq), 0)
    qry_i = lax.broadcasted_iota(jnp.int32, (tq, tq), 1)
    diag_keep = key_i <= qry_i
    keep2 = jnp.concatenate([diag_keep, diag_keep], axis=1)
    neg = -1e30
    qscale = FOX_HEAD_DIM ** -0.5 * LOG2E
    zero = jnp.zeros((), BF16)

    def stage():
        ka_ref[:, :LANES] = k_ref[0]
        ka_ref[:, LANES:] = fbm_ref[0]
        v_t = v_ref[0].astype(F32).T.astype(BF16)
        for hh in range(2):
            vt_ref[hh, :FOX_HEAD_DIM, :] = v_t[hh * FOX_HEAD_DIM:(hh + 1) * FOX_HEAD_DIM]
            vt_ref[hh, FOX_HEAD_DIM:, :] = jnp.ones((FOX_ONES_ROWS, seq), BF16)

    def scores(j):
        rows = slice(j * tq, (j + 1) * tq)
        n_off = j * tq
        q = (q_ref[0, rows, :].astype(F32) * qscale).astype(BF16)
        fa = fa_ref[0, rows, :]
        qa = []
        for hh in range(2):
            hmask = (lane // FOX_HEAD_DIM) == hh
            fmask = (lane // FOX_SLOT) == (2 * pair + hh)
            qa.append(jnp.concatenate(
                [jnp.where(hmask, q, zero), jnp.where(fmask, fa, zero)], axis=1))
        qa = jnp.concatenate(qa, axis=0)
        s_diag = jnp.where(keep2, _dot_nt(ka_ref[rows, :], qa), neg)
        m = jnp.max(s_diag, axis=0, keepdims=True)
        s_off = None
        if n_off:
            s_off = _dot_nt(ka_ref[:n_off, :], qa)
            m = jnp.maximum(m, jnp.max(s_off, axis=0, keepdims=True))
        return s_diag, s_off, m

    def probs(s_diag, s_off, m):
        p_off = None if s_off is None else jnp.exp2(s_off - m).astype(BF16)
        return jnp.exp2(s_diag - m).astype(BF16), p_off

    def weighted_values(j, p_diag, p_off):
        rows = slice(j * tq, (j + 1) * tq)
        n_off = j * tq
        outs = []
        for hh in range(2):
            qs = slice(hh * tq, (hh + 1) * tq)
            o_t = _dot(vt_ref[hh, :, rows], p_diag[:, qs])
            if n_off:
                o_t = o_t + _dot(vt_ref[hh, :, :n_off], p_off[:, qs])
            outs.append(o_t[:FOX_HEAD_DIM] * (1.0 / o_t[FOX_HEAD_DIM:FOX_HEAD_DIM + 1]))
        o = jnp.concatenate(outs, axis=0).T
        g = g_ref[0, rows, :].astype(F32)
        o_ref[0, rows, :] = (o * g * (1.0 / (1.0 + jnp.exp(-g)))).astype(BF16)

    stage()
    n_blk = seq // tq
    s_next = scores(0)
    p_prev = None
    for j in range(n_blk + 1):
        s_cur, s_next = s_next, (scores(j + 1) if j + 1 < n_blk else None)
        p_cur = probs(*s_cur) if j < n_blk else None
        if j > 0:
            weighted_values(j - 1, *p_prev)
        p_prev = p_cur


def _fox(proj, fa, fbm):
    bsz, seq, _ = proj.shape
    npair = D_FOX // LANES
    base = 4 * D_HGRN // LANES

    def col(j):
        return pl.BlockSpec((1, seq, LANES), lambda b, h: (b, 0, base + j * npair + h))

    gate = pl.BlockSpec((1, seq, LANES), lambda b, h: (b, 0, 0))
    return pl.pallas_call(
        _fox_kernel,
        grid=(bsz, npair),
        in_specs=[col(0), col(1), col(2), col(3), gate, gate],
        out_specs=pl.BlockSpec((1, seq, LANES), lambda b, h: (b, 0, h)),
        out_shape=jax.ShapeDtypeStruct((bsz, seq, D_FOX), BF16),
        scratch_shapes=[pltpu.VMEM((seq, 2 * LANES), BF16),
                        pltpu.VMEM((2, FOX_HEAD_DIM + FOX_ONES_ROWS, seq), BF16)],
        compiler_params=pltpu.CompilerParams(
            dimension_semantics=("parallel", "parallel"), vmem_limit_bytes=VMEM_LIMIT),
        name="fox",
    )(proj, proj, proj, proj, fa, fbm)


def _outproj_kernel(final, yh_ref, yf_ref, h_ref, p_ref, woh_ref, wof_ref, wp_ref, wg_ref,
                    fnw_ref, o_ref):
    h = h_ref[...] + _dot(yh_ref[...], woh_ref[...]) + _dot(yf_ref[...], wof_ref[...])
    ple = _dot(p_ref[...].astype(BF16), wp_ref[...])
    gate = _dot(h.astype(BF16), wg_ref[...])
    h = h + (1.0 / (1.0 + jnp.exp(-gate))) * ple
    if final:
        h = h * lax.rsqrt(jnp.mean(h * h, axis=-1, keepdims=True) + NORM_EPS) * fnw_ref[...]
    o_ref[...] = h


def _outproj(final, yh, yf, h, p, w_out_h, w_out_f, w_ple, w_gate, fnw, tm=512):
    m = h.shape[0]

    def rows(width):
        return pl.BlockSpec((tm, width), lambda i: (i, 0))

    def whole(a):
        return pl.BlockSpec(a.shape, lambda i: (0, 0))

    return pl.pallas_call(
        functools.partial(_outproj_kernel, final),
        grid=(m // tm,),
        in_specs=[rows(D_HGRN), rows(D_FOX), rows(D_MODEL), rows(PLE_DIM),
                  whole(w_out_h), whole(w_out_f), whole(w_ple), whole(w_gate), whole(fnw)],
        out_specs=rows(D_MODEL),
        out_shape=jax.ShapeDtypeStruct((m, D_MODEL), F32),
        compiler_params=pltpu.CompilerParams(
            dimension_semantics=("parallel",), vmem_limit_bytes=VMEM_LIMIT),
        name="outproj",
    )(yh, yf, h, p, w_out_h, w_out_f, w_ple, w_gate, fnw)


def kernel(x, p, norm_w, w_in, fox_fb, hgrn_gn, hgrn_lb_logits, w_out, w_ple, w_ple_gate, final_norm_w):
    bsz, seq, _ = x.shape
    depth = w_in.shape[0]
    m = bsz * seq
    w_main = w_in[:, :, :D_MAIN].astype(BF16)
    w_fgate = jnp.pad(w_in[:, :, D_MAIN:], ((0, 0), (0, 0), (0, LANES - FOX_HEADS))).astype(BF16)
    fb = jnp.pad(fox_fb, ((0, 0), (0, LANES - FOX_HEADS)))
    w_out_b = w_out.astype(BF16)
    w_ple_b = w_ple.astype(BF16)
    w_gate_b = w_ple_gate.astype(BF16)
    fnw = final_norm_w.reshape(1, D_MODEL)

    h = x.reshape(m, D_MODEL)
    for i in range(depth):
        proj, fz = _inproj(h, norm_w[i].reshape(1, D_MODEL), w_main[i], w_fgate[i])
        proj = proj.reshape(bsz, seq, D_MAIN)
        fa, fbm = _foxprep(fz.reshape(bsz, seq, LANES), fb[i].reshape(1, LANES))
        yh = _hgrn(i, proj, hgrn_lb_logits, hgrn_gn[i].reshape(1, D_HGRN))
        yf = _fox(proj, fa, fbm)
        h = _outproj(i == depth - 1, yh.reshape(m, D_HGRN), yf.reshape(m, D_FOX), h,
                     p[i].reshape(m, PLE_DIM), w_out_b[i, :D_HGRN], w_out_b[i, D_HGRN:],
                     w_ple_b[i], w_gate_b[i], fnw)
    return h.reshape(bsz, seq, D_MODEL)
```

```python
import functools

import jax
import jax.numpy as jnp
from jax import lax
from jax.experimental import pallas as pl
from jax.experimental.pallas import tpu as pltpu

D_MODEL = 1024
PLE_DIM = 256
D_HGRN = 1024
D_FOX = 1024
HGRN_HEAD_DIM = 128
HGRN_HEADS = D_HGRN // HGRN_HEAD_DIM
FOX_HEAD_DIM = 64
FOX_HEADS = D_FOX // FOX_HEAD_DIM
NORM_EPS = 1e-6
D_MAIN = 4 * D_HGRN + 4 * D_FOX

LANES = 128
HGRN_SUB = 32
HGRN_SLAB = 128
FOX_TQ = 256
FOX_SLOT = 8
FOX_ONES_ROWS = 16
LOG2E = 1.4426950408889634
HGRN_FAST_MIN_LOG_DECAY = -60.0
VMEM_LIMIT = 48 * 1024 * 1024

F32 = jnp.float32
BF16 = jnp.bfloat16


def _dot(a, b):
    return jnp.dot(a, b, preferred_element_type=F32)


def _dot_nt(a, b):
    return lax.dot_general(a, b, (((1,), (1,)), ((), ())), preferred_element_type=F32)


def _split_bf16(x, n):
    parts = []
    r = x
    for _ in range(n - 1):
        hi = r.astype(BF16)
        parts.append(hi)
        r = r - hi.astype(F32)
    parts.append(r.astype(BF16))
    return parts


def _dot_01_lhs(m01, x, n):
    acc = None
    for part in _split_bf16(x, n):
        t = _dot(m01, part)
        acc = t if acc is None else acc + t
    return acc


def _log_sigmoid(z):
    return jnp.minimum(z, 0.0) - jnp.log1p(jnp.exp(-jnp.abs(z)))


def _rmsnorm(x, w):
    return x * lax.rsqrt(jnp.mean(x * x, axis=-1, keepdims=True) + NORM_EPS) * w


def _prenorm_kernel(x_ref, nw_ref, u_ref):
    u_ref[...] = _rmsnorm(x_ref[...], nw_ref[...]).astype(BF16)


def _prenorm(x, nw, tm=1024):
    m = x.shape[0]
    return pl.pallas_call(
        _prenorm_kernel,
        grid=(m // tm,),
        in_specs=[pl.BlockSpec((tm, D_MODEL), lambda i: (i, 0)), pl.BlockSpec((1, D_MODEL), lambda i: (0, 0))],
        out_specs=pl.BlockSpec((tm, D_MODEL), lambda i: (i, 0)),
        out_shape=jax.ShapeDtypeStruct((m, D_MODEL), BF16),
        compiler_params=pltpu.CompilerParams(
            dimension_semantics=("parallel",), vmem_limit_bytes=VMEM_LIMIT),
        name="prenorm",
    )(x, nw)


def _inproj_kernel(u_ref, w_ref, wz_ref, proj_ref, fz_ref):
    @pl.when(pl.program_id(1) == 0)
    def _():
        fz_ref[...] = _dot(u_ref[...], wz_ref[...])

    proj_ref[...] = _dot(u_ref[...], w_ref[...]).astype(BF16)


def _inproj(u, w_main, w_gate, tm=1024, tn=2048):
    m = u.shape[0]
    return pl.pallas_call(
        _inproj_kernel,
        grid=(m // tm, D_MAIN // tn),
        in_specs=[
            pl.BlockSpec((tm, D_MODEL), lambda i, j: (i, 0)),
            pl.BlockSpec((D_MODEL, tn), lambda i, j: (0, j)),
            pl.BlockSpec((D_MODEL, LANES), lambda i, j: (0, 0)),
        ],
        out_specs=[
            pl.BlockSpec((tm, tn), lambda i, j: (i, j)),
            pl.BlockSpec((tm, LANES), lambda i, j: (i, 0)),
        ],
        out_shape=[
            jax.ShapeDtypeStruct((m, D_MAIN), BF16),
            jax.ShapeDtypeStruct((m, LANES), F32),
        ],
        compiler_params=pltpu.CompilerParams(
            dimension_semantics=("parallel", "arbitrary"), vmem_limit_bytes=VMEM_LIMIT),
        name="inproj",
    )(u, w_main, w_gate)


def _hgrn_kernel(layer, q_ref, f_ref, i_ref, g_ref, lbl_ref, gn_ref, o_ref,
                 qf_ref, oi_ref, ds_ref, dec_ref, qs_ref, ks_ref, bs_ref, vs_ref, flag_ref):
    seq = q_ref.shape[1]
    n_slab = seq // HGRN_SLAB
    n_sub = HGRN_SLAB // HGRN_SUB
    n_chunk = seq // HGRN_SUB

    if layer > 0:
        logits = lbl_ref[...]
        ex = jnp.exp(logits - jnp.max(logits, axis=0, keepdims=True))
        sm = ex / jnp.sum(ex, axis=0, keepdims=True)
        lb = jnp.sum(sm[1:layer + 1], axis=0, keepdims=True)
        log_lb = jnp.log(lb)
        log_1m_lb = jnp.log1p(-lb)
        one_m_lb = 1.0 - lb

    row = lax.broadcasted_iota(jnp.int32, (HGRN_SLAB, HGRN_SLAB), 0)
    col = lax.broadcasted_iota(jnp.int32, (HGRN_SLAB, HGRN_SLAB), 1)
    causal = (row >= col) & ((row // HGRN_SUB) == (col // HGRN_SUB))
    tril = jnp.where(causal, 1.0, 0.0).astype(BF16)
    row_chunk = lax.broadcasted_iota(jnp.int32, (HGRN_SLAB, HGRN_HEAD_DIM), 0) // HGRN_SUB
    scale = HGRN_HEAD_DIM ** -0.5
    gn = gn_ref[...]

    def slab_rows(si):
        return pl.ds(pl.multiple_of(si * HGRN_SLAB, HGRN_SLAB), HGRN_SLAB)

    def decays(rows):
        fz = f_ref[0, rows, :].astype(F32)
        e = jnp.exp(-jnp.abs(fz))
        r = 1.0 / (1.0 + e)
        log_sig = jnp.minimum(fz, 0.0) + jnp.log(r)
        sig_neg = jnp.where(fz >= 0.0, e * r, r)
        if layer > 0:
            bb = log_1m_lb + log_sig
            log_f = jnp.maximum(log_lb, bb) + jnp.log(1.0 + jnp.exp(-jnp.abs(log_lb - bb)))
            k = one_m_lb * sig_neg
        else:
            log_f = log_sig
            k = sig_neg
        hi, lo = _split_bf16(log_f * LOG2E, 2)
        csum = _dot(tril, jnp.concatenate([hi, lo], axis=1))
        return k, csum[:, :HGRN_HEAD_DIM] + csum[:, HGRN_HEAD_DIM:]

    def pass1_a(si):
        return decays(pl.ds(si * HGRN_SLAB, HGRN_SLAB))

    def pass1_b(si, k, b):
        rows = pl.ds(si * HGRN_SLAB, HGRN_SLAB)
        b_last = jnp.concatenate(
            [jnp.broadcast_to(b[(c + 1) * HGRN_SUB - 1:(c + 1) * HGRN_SUB, :], (HGRN_SUB, HGRN_HEAD_DIM))
             for c in range(n_sub)], axis=0)
        q = q_ref[0, rows, :].astype(F32) * scale
        qf = (q * jnp.exp2(b)).astype(BF16)
        kf = (k * jnp.exp2(-b)).astype(BF16)
        kb = (k * jnp.exp2(b_last - b)).astype(BF16)
        qf_ref[rows, :] = qf
        return b, kb, _dot_nt(qf, kf)

    def pass1_c(si, b, kb, scores):
        rows = pl.ds(si * HGRN_SLAB, HGRN_SLAB)
        v = i_ref[0, rows, :]
        oi_ref[rows, :] = _dot(jnp.where(causal, scores, 0.0).astype(BF16), v)
        v_t = v.astype(F32).T.astype(BF16)
        zero = jnp.zeros((), BF16)
        for c in range(0, n_sub, 2):
            kb2 = jnp.concatenate([jnp.where(row_chunk == c, kb, zero),
                                   jnp.where(row_chunk == c + 1, kb, zero)], axis=1)
            ds2 = _dot(v_t, kb2)
            ds_ref[si * n_sub + c] = ds2[:, :HGRN_HEAD_DIM]
            ds_ref[si * n_sub + c + 1] = ds2[:, HGRN_HEAD_DIM:]
        for c in range(n_sub):
            ci = si * n_sub + c
            dec_ref[pl.ds(ci, 1), :] = jnp.exp2(b[(c + 1) * HGRN_SUB - 1:(c + 1) * HGRN_SUB, :])
        flag_ref[si] = (jnp.min(b) < HGRN_FAST_MIN_LOG_DECAY * LOG2E).astype(jnp.int32)

    def redo_slab(si, carry):
        @pl.when(flag_ref[si] != 0)
        def _():
            rows = slab_rows(si)
            r0 = pl.multiple_of(si * HGRN_SLAB, HGRN_SLAB)
            k, b = decays(rows)
            qs_ref[...] = q_ref[0, rows, :].astype(F32) * scale
            ks_ref[...] = k
            bs_ref[...] = b
            vs_ref[...] = i_ref[0, rows, :].astype(F32)
            sub_row = lax.broadcasted_iota(jnp.int32, (HGRN_SUB, 1), 0)

            def one_row(t, c):
                c0 = pl.multiple_of((t // HGRN_SUB) * HGRN_SUB, HGRN_SUB)
                crow = pl.ds(c0, HGRN_SUB)
                d = bs_ref[pl.ds(t, 1), :] - bs_ref[crow, :]
                keep = (sub_row + c0) <= t
                w = jnp.where(keep, jnp.exp2(jnp.minimum(d, 0.0)), 0.0)
                sc = jnp.sum(qs_ref[pl.ds(t, 1), :] * ks_ref[crow, :] * w, axis=1, keepdims=True)
                oi_ref[pl.ds(r0 + t, 1), :] = jnp.sum(sc * vs_ref[crow, :], axis=0, keepdims=True)
                return c

            lax.fori_loop(0, HGRN_SLAB, one_row, 0)
        return carry

    def redo():
        lax.fori_loop(0, n_slab, redo_slab, 0)

    def pass2(ci, state_t):
        crow = pl.ds(ci * HGRN_SUB, HGRN_SUB)
        o = oi_ref[crow, :] + _dot_nt(qf_ref[crow, :], state_t.astype(BF16))
        o = o * lax.rsqrt(jnp.mean(o * o, axis=-1, keepdims=True) + NORM_EPS) * gn
        g = g_ref[0, crow, :].astype(F32)
        o_ref[0, crow, :] = (o * g * (1.0 / (1.0 + jnp.exp(-g)))).astype(BF16)
        return state_t * dec_ref[pl.ds(ci, 1), :] + ds_ref[ci]

    stage_a = {0: pass1_a(0)}
    stage_b = {}
    for step in range(n_slab + 1):
        if step + 1 < n_slab:
            stage_a[step + 1] = pass1_a(step + 1)
        if step < n_slab:
            stage_b[step] = pass1_b(step, *stage_a.pop(step))
        if step >= 1:
            pass1_c(step - 1, *stage_b.pop(step - 1))
    redo()
    state_t = jnp.zeros((HGRN_HEAD_DIM, HGRN_HEAD_DIM), F32)
    for ci in range(n_chunk):
        state_t = pass2(ci, state_t)


def _hgrn(layer, proj, lb_logits, gn):
    bsz, seq, _ = proj.shape
    depth = lb_logits.shape[0]
    nh = HGRN_HEADS

    def col(j):
        return pl.BlockSpec((1, seq, HGRN_HEAD_DIM), lambda b, h: (b, 0, j * nh + h))

    slab = pltpu.VMEM((HGRN_SLAB, HGRN_HEAD_DIM), F32)
    return pl.pallas_call(
        functools.partial(_hgrn_kernel, layer),
        grid=(bsz, nh),
        in_specs=[col(0), col(1), col(2), col(3),
                  pl.BlockSpec((depth, HGRN_HEAD_DIM), lambda b, h: (0, h)),
                  pl.BlockSpec((1, HGRN_HEAD_DIM), lambda b, h: (0, h))],
        out_specs=pl.BlockSpec((1, seq, HGRN_HEAD_DIM), lambda b, h: (b, 0, h)),
        out_shape=jax.ShapeDtypeStruct((bsz, seq, D_HGRN), BF16),
        scratch_shapes=[
            pltpu.VMEM((seq, HGRN_HEAD_DIM), BF16),
            pltpu.VMEM((seq, HGRN_HEAD_DIM), F32),
            pltpu.VMEM((seq // HGRN_SUB, HGRN_HEAD_DIM, HGRN_HEAD_DIM), F32),
            pltpu.VMEM((seq // HGRN_SUB, HGRN_HEAD_DIM), F32),
            slab, slab, slab, slab,
            pltpu.SMEM((seq // HGRN_SLAB,), jnp.int32),
        ],
        compiler_params=pltpu.CompilerParams(
            dimension_semantics=("parallel", "parallel"), vmem_limit_bytes=VMEM_LIMIT),
        name="hgrn2",
    )(proj, proj, proj, proj, lb_logits, gn)


def _foxprep_kernel(fz_ref, fb_ref, fa_ref, fbm_ref):
    seq = fz_ref.shape[1]
    blk = LANES
    row = lax.broadcasted_iota(jnp.int32, (blk, blk), 0)
    col = lax.broadcasted_iota(jnp.int32, (blk, blk), 1)
    tril = jnp.where(row >= col, 1.0, 0.0).astype(BF16)
    sel = [jnp.where((col == FOX_SLOT * row + j) & (row < FOX_HEADS), 1.0, 0.0).astype(BF16)
           for j in range(6)]
    lane = lax.broadcasted_iota(jnp.int32, (1, blk), 1) % FOX_SLOT
    ones_a = jnp.where((lane >= 3) & (lane < 6), 1.0, 0.0)
    ones_b = jnp.where(lane < 3, 1.0, 0.0)
    fb = fb_ref[...]
    carry = jnp.zeros((1, blk), F32)
    for i in range(seq // blk):
        rows = slice(i * blk, (i + 1) * blk)
        log_f = _log_sigmoid(fz_ref[0, rows, :] + fb)
        csum = _dot_01_lhs(tril, log_f, 3) + carry
        carry = csum[blk - 1:blk, :]
        parts = _split_bf16(csum * LOG2E, 3)
        fa = ones_a
        fbm = ones_b
        for j in range(3):
            fa = fa + _dot(parts[j], sel[j])
            fbm = fbm - _dot(parts[j], sel[3 + j])
        fa_ref[0, rows, :] = fa.astype(BF16)
        fbm_ref[0, rows, :] = fbm.astype(BF16)


def _foxprep(fz, fb):
    bsz, seq, _ = fz.shape
    spec = pl.BlockSpec((1, seq, LANES), lambda b: (b, 0, 0))
    return pl.pallas_call(
        _foxprep_kernel,
        grid=(bsz,),
        in_specs=[spec, pl.BlockSpec((1, LANES), lambda b: (0, 0))],
        out_specs=[spec, spec],
        out_shape=[jax.ShapeDtypeStruct((bsz, seq, LANES), BF16)] * 2,
        compiler_params=pltpu.CompilerParams(
            dimension_semantics=("parallel",), vmem_limit_bytes=VMEM_LIMIT),
        name="foxprep",
    )(fz, fb)


def _fox_kernel(q_ref, k_ref, v_ref, g_ref, fa_ref, fbm_ref, o_ref, ka_ref, vt_ref,
                s0_ref, s1_ref, p0_ref, p1_ref):
    seq = q_ref.shape[1]
    tq = FOX_TQ
    pair = pl.program_id(1)
    lane = lax.broadcasted_iota(jnp.int32, (1, LANES), 1)
    key_i = lax.broadcasted_iota(jnp.int32, (tq, tq), 0)
    qry_i = lax.broadcasted_iota(jnp.int32, (tq, tq), 1)
    diag_keep = key_i <= qry_i
    keep2 = jnp.concatenate([diag_keep, diag_keep], axis=1)
    neg = -1e30
    qscale = FOX_HEAD_DIM ** -0.5 * LOG2E
    zero = jnp.zeros((), BF16)
    s_bufs = (s0_ref, s1_ref)
    p_bufs = (p0_ref, p1_ref)

    def stage():
        ka_ref[:, :LANES] = k_ref[0]
        ka_ref[:, LANES:] = fbm_ref[0]
        v_t = v_ref[0].astype(F32).T.astype(BF16)
        for hh in range(2):
            vt_ref[hh, :FOX_HEAD_DIM, :] = v_t[hh * FOX_HEAD_DIM:(hh + 1) * FOX_HEAD_DIM]
            vt_ref[hh, FOX_HEAD_DIM:, :] = jnp.ones((FOX_ONES_ROWS, seq), BF16)

    def scores(j):
        rows = slice(j * tq, (j + 1) * tq)
        n_off = j * tq
        q = (q_ref[0, rows, :].astype(F32) * qscale).astype(BF16)
        fa = fa_ref[0, rows, :]
        qa = []
        for hh in range(2):
            hmask = (lane // FOX_HEAD_DIM) == hh
            fmask = (lane // FOX_SLOT) == (2 * pair + hh)
            qa.append(jnp.concatenate(
                [jnp.where(hmask, q, zero), jnp.where(fmask, fa, zero)], axis=1))
        qa = jnp.concatenate(qa, axis=0)
        s_buf = s_bufs[j % 2]
        s_diag = jnp.where(keep2, _dot_nt(ka_ref[rows, :], qa), neg)
        s_buf[rows, :] = s_diag
        m = jnp.max(s_diag, axis=0, keepdims=True)
        if n_off:
            s_off = _dot_nt(ka_ref[:n_off, :], qa)
            s_buf[:n_off, :] = s_off
            m = jnp.maximum(m, jnp.max(s_off, axis=0, keepdims=True))
        return j, m

    def probs(j, m):
        keys = slice(0, (j + 1) * tq)
        p_bufs[j % 2][keys, :] = jnp.exp2(s_bufs[j % 2][keys, :] - m).astype(BF16)
        return (j,)

    def weighted_values(j):
        rows = slice(j * tq, (j + 1) * tq)
        keys = slice(0, (j + 1) * tq)
        outs = []
        for hh in range(2):
            qs = slice(hh * tq, (hh + 1) * tq)
            o_t = _dot(vt_ref[hh, :, keys], p_bufs[j % 2][keys, qs])
            outs.append(o_t[:FOX_HEAD_DIM] * (1.0 / o_t[FOX_HEAD_DIM:FOX_HEAD_DIM + 1]))
        o = jnp.concatenate(outs, axis=0).T
        g = g_ref[0, rows, :].astype(F32)
        o_ref[0, rows, :] = (o * g * (1.0 / (1.0 + jnp.exp(-g)))).astype(BF16)

    stage()
    n_blk = seq // tq
    s_next = scores(0)
    p_prev = None
    for j in range(n_blk + 1):
        s_cur, s_next = s_next, (scores(j + 1) if j + 1 < n_blk else None)
        p_cur = probs(*s_cur) if j < n_blk else None
        if j > 0:
            weighted_values(*p_prev)
        p_prev = p_cur


def _fox(proj, fa, fbm):
    bsz, seq, _ = proj.shape
    npair = D_FOX // LANES
    base = 4 * D_HGRN // LANES

    def col(j):
        return pl.BlockSpec((1, seq, LANES), lambda b, h: (b, 0, base + j * npair + h))

    gate = pl.BlockSpec((1, seq, LANES), lambda b, h: (b, 0, 0))
    return pl.pallas_call(
        _fox_kernel,
        grid=(bsz, npair),
        in_specs=[col(0), col(1), col(2), col(3), gate, gate],
        out_specs=pl.BlockSpec((1, seq, LANES), lambda b, h: (b, 0, h)),
        out_shape=jax.ShapeDtypeStruct((bsz, seq, D_FOX), BF16),
        scratch_shapes=[pltpu.VMEM((seq, 2 * LANES), BF16),
                        pltpu.VMEM((2, FOX_HEAD_DIM + FOX_ONES_ROWS, seq), BF16),
                        pltpu.VMEM((seq, 2 * FOX_TQ), F32), pltpu.VMEM((seq, 2 * FOX_TQ), F32),
                        pltpu.VMEM((seq, 2 * FOX_TQ), BF16), pltpu.VMEM((seq, 2 * FOX_TQ), BF16)],
        compiler_params=pltpu.CompilerParams(
            dimension_semantics=("parallel", "parallel"), vmem_limit_bytes=VMEM_LIMIT),
        name="fox",
    )(proj, proj, proj, proj, fa, fbm)


def _outproj_kernel(final, yh_ref, yf_ref, h_ref, p_ref, woh_ref, wof_ref, wp_ref, wg_ref,
                    nw_ref, o_ref, *u_ref):
    h = h_ref[...] + _dot(yh_ref[...], woh_ref[...]) + _dot(yf_ref[...], wof_ref[...])
    ple = _dot(p_ref[...].astype(BF16), wp_ref[...])
    gate = _dot(h.astype(BF16), wg_ref[...])
    h = h + (1.0 / (1.0 + jnp.exp(-gate))) * ple
    if final:
        o_ref[...] = _rmsnorm(h, nw_ref[...])
    else:
        o_ref[...] = h
        u_ref[0][...] = _rmsnorm(h, nw_ref[...]).astype(BF16)


def _outproj(final, yh, yf, h, p, w_out_h, w_out_f, w_ple, w_gate, nw, tm=512):
    m = h.shape[0]

    def rows(width):
        return pl.BlockSpec((tm, width), lambda i: (i, 0))

    def whole(a):
        return pl.BlockSpec(a.shape, lambda i: (0, 0))

    out_specs = [rows(D_MODEL)]
    out_shape = [jax.ShapeDtypeStruct((m, D_MODEL), F32)]
    if not final:
        out_specs.append(rows(D_MODEL))
        out_shape.append(jax.ShapeDtypeStruct((m, D_MODEL), BF16))
    return pl.pallas_call(
        functools.partial(_outproj_kernel, final),
        grid=(m // tm,),
        in_specs=[rows(D_HGRN), rows(D_FOX), rows(D_MODEL), rows(PLE_DIM),
                  whole(w_out_h), whole(w_out_f), whole(w_ple), whole(w_gate), whole(nw)],
        out_specs=out_specs,
        out_shape=out_shape,
        compiler_params=pltpu.CompilerParams(
            dimension_semantics=("parallel",), vmem_limit_bytes=VMEM_LIMIT),
        name="outproj",
    )(yh, yf, h, p, w_out_h, w_out_f, w_ple, w_gate, nw)


def kernel(x, p, norm_w, w_in, fox_fb, hgrn_gn, hgrn_lb_logits, w_out, w_ple, w_ple_gate, final_norm_w):
    bsz, seq, _ = x.shape
    depth = w_in.shape[0]
    m = bsz * seq
    w_main = w_in[:, :, :D_MAIN].astype(BF16)
    w_fgate = jnp.pad(w_in[:, :, D_MAIN:], ((0, 0), (0, 0), (0, LANES - FOX_HEADS))).astype(BF16)
    fb = jnp.pad(fox_fb, ((0, 0), (0, LANES - FOX_HEADS)))
    w_out_b = w_out.astype(BF16)
    w_ple_b = w_ple.astype(BF16)
    w_gate_b = w_ple_gate.astype(BF16)
    fnw = final_norm_w.reshape(1, D_MODEL)

    h = x.reshape(m, D_MODEL)
    u = _prenorm(h, norm_w[0].reshape(1, D_MODEL))
    for i in range(depth):
        final = i == depth - 1
        proj, fz = _inproj(u, w_main[i], w_fgate[i])
        proj = proj.reshape(bsz, seq, D_MAIN)
        fa, fbm = _foxprep(fz.reshape(bsz, seq, LANES), fb[i].reshape(1, LANES))
        yh = _hgrn(i, proj, hgrn_lb_logits, hgrn_gn[i].reshape(1, D_HGRN))
        yf = _fox(proj, fa, fbm)
        outs = _outproj(final, yh.reshape(m, D_HGRN), yf.reshape(m, D_FOX), h,
                        p[i].reshape(m, PLE_DIM), w_out_b[i, :D_HGRN], w_out_b[i, D_HGRN:],
                        w_ple_b[i], w_gate_b[i], fnw if final else norm_w[i + 1].reshape(1, D_MODEL))
        h = outs[0]
        if not final:
            u = outs[1]
    return h.reshape(bsz, seq, D_MODEL)
```

```python
import functools

import jax
import jax.numpy as jnp
from jax import lax
from jax.experimental import pallas as pl
from jax.experimental.pallas import tpu as pltpu

D_MODEL = 1024
PLE_DIM = 256
D_HGRN = 1024
D_FOX = 1024
HGRN_HEAD_DIM = 128
HGRN_HEADS = D_HGRN // HGRN_HEAD_DIM
FOX_HEAD_DIM = 64
FOX_HEADS = D_FOX // FOX_HEAD_DIM
NORM_EPS = 1e-6
D_MAIN = 4 * D_HGRN + 4 * D_FOX

LANES = 128
HGRN_SUB = 32
HGRN_PAIR = 2 * HGRN_SUB
HGRN_SLAB = 128
FOX_TQ = 256
FOX_SLOT = 8
FOX_ONES_ROWS = 16
LOG2E = 1.4426950408889634
HGRN_FAST_MIN_LOG_DECAY = -60.0
VMEM_LIMIT = 48 * 1024 * 1024

F32 = jnp.float32
BF16 = jnp.bfloat16


def _dot(a, b):
    return jnp.dot(a, b, preferred_element_type=F32)


def _dot_nt(a, b):
    return lax.dot_general(a, b, (((1,), (1,)), ((), ())), preferred_element_type=F32)


def _split_bf16(x, n):
    parts = []
    r = x
    for _ in range(n - 1):
        hi = r.astype(BF16)
        parts.append(hi)
        r = r - hi.astype(F32)
    parts.append(r.astype(BF16))
    return parts


def _dot_01_lhs(m01, x, n):
    acc = None
    for part in _split_bf16(x, n):
        t = _dot(m01, part)
        acc = t if acc is None else acc + t
    return acc


def _log_sigmoid(z):
    return jnp.minimum(z, 0.0) - jnp.log1p(jnp.exp(-jnp.abs(z)))


def _rmsnorm(x, w):
    return x * lax.rsqrt(jnp.mean(x * x, axis=-1, keepdims=True) + NORM_EPS) * w


def _prenorm_kernel(x_ref, nw_ref, u_ref):
    u_ref[...] = _rmsnorm(x_ref[...], nw_ref[...]).astype(BF16)


def _prenorm(x, nw, tm=1024):
    m = x.shape[0]
    return pl.pallas_call(
        _prenorm_kernel,
        grid=(m // tm,),
        in_specs=[pl.BlockSpec((tm, D_MODEL), lambda i: (i, 0)), pl.BlockSpec((1, D_MODEL), lambda i: (0, 0))],
        out_specs=pl.BlockSpec((tm, D_MODEL), lambda i: (i, 0)),
        out_shape=jax.ShapeDtypeStruct((m, D_MODEL), BF16),
        compiler_params=pltpu.CompilerParams(
            dimension_semantics=("parallel",), vmem_limit_bytes=VMEM_LIMIT),
        name="prenorm",
    )(x, nw)


def _inproj_kernel(u_ref, w_ref, wz_ref, proj_ref, fz_ref):
    @pl.when(pl.program_id(1) == 0)
    def _():
        fz_ref[...] = _dot(u_ref[...], wz_ref[...])

    proj_ref[...] = _dot(u_ref[...], w_ref[...]).astype(BF16)


def _inproj(u, w_main, w_gate, tm=1024, tn=2048):
    m = u.shape[0]
    return pl.pallas_call(
        _inproj_kernel,
        grid=(m // tm, D_MAIN // tn),
        in_specs=[
            pl.BlockSpec((tm, D_MODEL), lambda i, j: (i, 0)),
            pl.BlockSpec((D_MODEL, tn), lambda i, j: (0, j)),
            pl.BlockSpec((D_MODEL, LANES), lambda i, j: (0, 0)),
        ],
        out_specs=[
            pl.BlockSpec((tm, tn), lambda i, j: (i, j)),
            pl.BlockSpec((tm, LANES), lambda i, j: (i, 0)),
        ],
        out_shape=[
            jax.ShapeDtypeStruct((m, D_MAIN), BF16),
            jax.ShapeDtypeStruct((m, LANES), F32),
        ],
        compiler_params=pltpu.CompilerParams(
            dimension_semantics=("parallel", "arbitrary"), vmem_limit_bytes=VMEM_LIMIT),
        name="inproj",
    )(u, w_main, w_gate)


def _hgrn_kernel(layer, q_ref, f_ref, i_ref, g_ref, lbl_ref, gn_ref, o_ref,
                 qf_ref, oi_ref, ds_ref, dec_ref, qs_ref, ks_ref, bs_ref, vs_ref, flag_ref):
    seq = q_ref.shape[1]
    n_slab = seq // HGRN_SLAB
    n_sub = HGRN_SLAB // HGRN_SUB

    if layer > 0:
        logits = lbl_ref[...]
        ex = jnp.exp(logits - jnp.max(logits, axis=0, keepdims=True))
        sm = ex / jnp.sum(ex, axis=0, keepdims=True)
        lb = jnp.sum(sm[1:layer + 1], axis=0, keepdims=True)
        log_lb = jnp.log(lb)
        log_1m_lb = jnp.log1p(-lb)
        one_m_lb = 1.0 - lb

    row = lax.broadcasted_iota(jnp.int32, (HGRN_SLAB, HGRN_SLAB), 0)
    col = lax.broadcasted_iota(jnp.int32, (HGRN_SLAB, HGRN_SLAB), 1)
    causal = (row >= col) & ((row // HGRN_SUB) == (col // HGRN_SUB))
    tril = jnp.where(causal, 1.0, 0.0).astype(BF16)
    row_chunk = lax.broadcasted_iota(jnp.int32, (HGRN_SLAB, HGRN_HEAD_DIM), 0) // HGRN_SUB
    scale = HGRN_HEAD_DIM ** -0.5
    gn = gn_ref[...]

    def slab_rows(si):
        return pl.ds(pl.multiple_of(si * HGRN_SLAB, HGRN_SLAB), HGRN_SLAB)

    def decays(rows):
        fz = f_ref[0, rows, :].astype(F32)
        e = jnp.exp(-jnp.abs(fz))
        r = 1.0 / (1.0 + e)
        log_sig = jnp.minimum(fz, 0.0) + jnp.log(r)
        sig_neg = jnp.where(fz >= 0.0, e * r, r)
        if layer > 0:
            bb = log_1m_lb + log_sig
            log_f = jnp.maximum(log_lb, bb) + jnp.log(1.0 + jnp.exp(-jnp.abs(log_lb - bb)))
            k = one_m_lb * sig_neg
        else:
            log_f = log_sig
            k = sig_neg
        hi, lo = _split_bf16(log_f * LOG2E, 2)
        csum = _dot(tril, jnp.concatenate([hi, lo], axis=1))
        return k, csum[:, :HGRN_HEAD_DIM] + csum[:, HGRN_HEAD_DIM:]

    col_chunk = col // HGRN_SUB
    cross = (row // HGRN_SUB == col_chunk + 1) & (col_chunk % 2 == 0)
    row_pair = lax.broadcasted_iota(jnp.int32, (HGRN_SLAB, HGRN_HEAD_DIM), 0) // HGRN_PAIR
    n_pair = HGRN_SLAB // HGRN_PAIR

    def chunk_end(b, c):
        return b[(c + 1) * HGRN_SUB - 1:(c + 1) * HGRN_SUB, :]

    def per_chunk(vals):
        return jnp.concatenate([jnp.broadcast_to(x, (HGRN_SUB, HGRN_HEAD_DIM)) for x in vals], axis=0)

    def factors(rows, k, b):
        q = q_ref[0, rows, :].astype(F32) * scale
        qf = q * jnp.exp2(b)
        kf = (k * jnp.exp2(-b)).astype(BF16)
        kb = k * jnp.exp2(per_chunk([chunk_end(b, c) for c in range(n_sub)]) - b)
        return qf, kf, kb

    def pass1_a(si):
        return decays(pl.ds(si * HGRN_SLAB, HGRN_SLAB))

    def pass1_b(si, k, b):
        rows = pl.ds(si * HGRN_SLAB, HGRN_SLAB)
        qf, kf, kb = factors(rows, k, b)
        dec = [jnp.exp2(chunk_end(b, c)) for c in range(n_sub)]
        one = jnp.ones((1, HGRN_HEAD_DIM), F32)
        qf_ref[rows, :] = (qf * per_chunk([dec[c - 1] if c % 2 else one for c in range(n_sub)])).astype(BF16)
        kb_pair = (kb * per_chunk([one if c % 2 else dec[c + 1] for c in range(n_sub)])).astype(BF16)
        for pi in range(n_pair):
            dec_ref[pl.ds(si * n_pair + pi, 1), :] = dec[2 * pi] * dec[2 * pi + 1]
        flag_ref[si] = (jnp.min(b) < HGRN_FAST_MIN_LOG_DECAY * LOG2E).astype(jnp.int32)
        qf = qf.astype(BF16)
        return kb_pair, _dot_nt(qf, kf), _dot_nt(qf, kb.astype(BF16))

    def pass1_c(si, kb_pair, s_diag, s_cross):
        rows = pl.ds(si * HGRN_SLAB, HGRN_SLAB)
        v = i_ref[0, rows, :]
        scores = jnp.where(causal, s_diag, jnp.where(cross, s_cross, 0.0))
        oi_ref[rows, :] = _dot(scores.astype(BF16), v)
        v_t = v.astype(F32).T.astype(BF16)
        zero = jnp.zeros((), BF16)
        kb2 = jnp.concatenate([jnp.where(row_pair == pi, kb_pair, zero) for pi in range(n_pair)], axis=1)
        ds2 = _dot(v_t, kb2)
        for pi in range(n_pair):
            ds_ref[si * n_pair + pi] = ds2[:, pi * HGRN_HEAD_DIM:(pi + 1) * HGRN_HEAD_DIM]

    def redo_slab(si, carry):
        @pl.when(flag_ref[si] != 0)
        def _():
            rows = slab_rows(si)
            r0 = pl.multiple_of(si * HGRN_SLAB, HGRN_SLAB)
            k, b = decays(rows)
            qf, _, kb = factors(rows, k, b)
            s_cross = jnp.where(cross, _dot_nt(qf.astype(BF16), kb.astype(BF16)), 0.0)
            oi_ref[rows, :] = _dot(s_cross.astype(BF16), i_ref[0, rows, :])
            qs_ref[...] = q_ref[0, rows, :].astype(F32) * scale
            ks_ref[...] = k
            bs_ref[...] = b
            vs_ref[...] = i_ref[0, rows, :].astype(F32)
            sub_row = lax.broadcasted_iota(jnp.int32, (HGRN_SUB, 1), 0)

            def one_row(t, c):
                c0 = pl.multiple_of((t // HGRN_SUB) * HGRN_SUB, HGRN_SUB)
                crow = pl.ds(c0, HGRN_SUB)
                d = bs_ref[pl.ds(t, 1), :] - bs_ref[crow, :]
                keep = (sub_row + c0) <= t
                w = jnp.where(keep, jnp.exp2(jnp.minimum(d, 0.0)), 0.0)
                sc = jnp.sum(qs_ref[pl.ds(t, 1), :] * ks_ref[crow, :] * w, axis=1, keepdims=True)
                oi_ref[pl.ds(r0 + t, 1), :] += jnp.sum(sc * vs_ref[crow, :], axis=0, keepdims=True)
                return c

            lax.fori_loop(0, HGRN_SLAB, one_row, 0)
        return carry

    def redo():
        lax.fori_loop(0, n_slab, redo_slab, 0)

    def pass2(pi, state_t):
        prow = pl.ds(pi * HGRN_PAIR, HGRN_PAIR)
        o = oi_ref[prow, :] + _dot_nt(qf_ref[prow, :], state_t.astype(BF16))
        o = o * lax.rsqrt(jnp.mean(o * o, axis=-1, keepdims=True) + NORM_EPS) * gn
        g = g_ref[0, prow, :].astype(F32)
        o_ref[0, prow, :] = (o * g * (1.0 / (1.0 + jnp.exp(-g)))).astype(BF16)
        return state_t * dec_ref[pl.ds(pi, 1), :] + ds_ref[pi]

    stage_a = {0: pass1_a(0)}
    stage_b = {}
    for step in range(n_slab + 1):
        if step + 1 < n_slab:
            stage_a[step + 1] = pass1_a(step + 1)
        if step < n_slab:
            stage_b[step] = pass1_b(step, *stage_a.pop(step))
        if step >= 1:
            pass1_c(step - 1, *stage_b.pop(step - 1))
    redo()
    state_t = jnp.zeros((HGRN_HEAD_DIM, HGRN_HEAD_DIM), F32)
    for pi in range(seq // HGRN_PAIR):
        state_t = pass2(pi, state_t)


def _hgrn(layer, proj, lb_logits, gn):
    bsz, seq, _ = proj.shape
    depth = lb_logits.shape[0]
    nh = HGRN_HEADS

    def col(j):
        return pl.BlockSpec((1, seq, HGRN_HEAD_DIM), lambda b, h: (b, 0, j * nh + h))

    slab = pltpu.VMEM((HGRN_SLAB, HGRN_HEAD_DIM), F32)
    return pl.pallas_call(
        functools.partial(_hgrn_kernel, layer),
        grid=(bsz, nh),
        in_specs=[col(0), col(1), col(2), col(3),
                  pl.BlockSpec((depth, HGRN_HEAD_DIM), lambda b, h: (0, h)),
                  pl.BlockSpec((1, HGRN_HEAD_DIM), lambda b, h: (0, h))],
        out_specs=pl.BlockSpec((1, seq, HGRN_HEAD_DIM), lambda b, h: (b, 0, h)),
        out_shape=jax.ShapeDtypeStruct((bsz, seq, D_HGRN), BF16),
        scratch_shapes=[
            pltpu.VMEM((seq, HGRN_HEAD_DIM), BF16),
            pltpu.VMEM((seq, HGRN_HEAD_DIM), F32),
            pltpu.VMEM((seq // HGRN_PAIR, HGRN_HEAD_DIM, HGRN_HEAD_DIM), F32),
            pltpu.VMEM((seq // HGRN_PAIR, HGRN_HEAD_DIM), F32),
            slab, slab, slab, slab,
            pltpu.SMEM((seq // HGRN_SLAB,), jnp.int32),
        ],
        compiler_params=pltpu.CompilerParams(
            dimension_semantics=("parallel", "parallel"), vmem_limit_bytes=VMEM_LIMIT),
        name="hgrn2",
    )(proj, proj, proj, proj, lb_logits, gn)


def _foxprep_kernel(fz_ref, fb_ref, fa_ref, fbm_ref):
    seq = fz_ref.shape[1]
    blk = LANES
    row = lax.broadcasted_iota(jnp.int32, (blk, blk), 0)
    col = lax.broadcasted_iota(jnp.int32, (blk, blk), 1)
    tril = jnp.where(row >= col, 1.0, 0.0).astype(BF16)
    sel = [jnp.where((col == FOX_SLOT * row + j) & (row < FOX_HEADS), 1.0, 0.0).astype(BF16)
           for j in range(6)]
    lane = lax.broadcasted_iota(jnp.int32, (1, blk), 1) % FOX_SLOT
    ones_a = jnp.where((lane >= 3) & (lane < 6), 1.0, 0.0)
    ones_b = jnp.where(lane < 3, 1.0, 0.0)
    fb = fb_ref[...]
    carry = jnp.zeros((1, blk), F32)
    for i in range(seq // blk):
        rows = slice(i * blk, (i + 1) * blk)
        log_f = _log_sigmoid(fz_ref[0, rows, :] + fb)
        csum = _dot_01_lhs(tril, log_f, 3) + carry
        carry = csum[blk - 1:blk, :]
        parts = _split_bf16(csum * LOG2E, 3)
        fa = ones_a
        fbm = ones_b
        for j in range(3):
            fa = fa + _dot(parts[j], sel[j])
            fbm = fbm - _dot(parts[j], sel[3 + j])
        fa_ref[0, rows, :] = fa.astype(BF16)
        fbm_ref[0, rows, :] = fbm.astype(BF16)


def _foxprep(fz, fb):
    bsz, seq, _ = fz.shape
    spec = pl.BlockSpec((1, seq, LANES), lambda b: (b, 0, 0))
    return pl.pallas_call(
        _foxprep_kernel,
        grid=(bsz,),
        in_specs=[spec, pl.BlockSpec((1, LANES), lambda b: (0, 0))],
        out_specs=[spec, spec],
        out_shape=[jax.ShapeDtypeStruct((bsz, seq, LANES), BF16)] * 2,
        compiler_params=pltpu.CompilerParams(
            dimension_semantics=("parallel",), vmem_limit_bytes=VMEM_LIMIT),
        name="foxprep",
    )(fz, fb)


def _fox_kernel(q_ref, k_ref, v_ref, g_ref, fa_ref, fbm_ref, o_ref, ka_ref, vt_ref,
                s0_ref, s1_ref, p0_ref, p1_ref):
    seq = q_ref.shape[1]
    tq = FOX_TQ
    pair = pl.program_id(1)
    lane = lax.broadcasted_iota(jnp.int32, (1, LANES), 1)
    key_i = lax.broadcasted_iota(jnp.int32, (tq, tq), 0)
    qry_i = lax.broadcasted_iota(jnp.int32, (tq, tq), 1)
    diag_keep = key_i <= qry_i
    keep2 = jnp.concatenate([diag_keep, diag_keep], axis=1)
    neg = -1e30
    qscale = FOX_HEAD_DIM ** -0.5 * LOG2E
    zero = jnp.zeros((), BF16)
    s_bufs = (s0_ref, s1_ref)
    p_bufs = (p0_ref, p1_ref)

    def stage():
        ka_ref[:, :LANES] = k_ref[0]
        ka_ref[:, LANES:] = fbm_ref[0]
        v_t = v_ref[0].astype(F32).T.astype(BF16)
        for hh in range(2):
            vt_ref[hh, :FOX_HEAD_DIM, :] = v_t[hh * FOX_HEAD_DIM:(hh + 1) * FOX_HEAD_DIM]
            vt_ref[hh, FOX_HEAD_DIM:, :] = jnp.ones((FOX_ONES_ROWS, seq), BF16)

    def scores(j):
        rows = slice(j * tq, (j + 1) * tq)
        n_off = j * tq
        q = (q_ref[0, rows, :].astype(F32) * qscale).astype(BF16)
        fa = fa_ref[0, rows, :]
        qa = []
        for hh in range(2):
            hmask = (lane // FOX_HEAD_DIM) == hh
            fmask = (lane // FOX_SLOT) == (2 * pair + hh)
            qa.append(jnp.concatenate(
                [jnp.where(hmask, q, zero), jnp.where(fmask, fa, zero)], axis=1))
        qa = jnp.concatenate(qa, axis=0)
        s_buf = s_bufs[j % 2]
        s_diag = jnp.where(keep2, _dot_nt(ka_ref[rows, :], qa), neg)
        s_buf[rows, :] = s_diag
        m = jnp.max(s_diag, axis=0, keepdims=True)
        if n_off:
            s_off = _dot_nt(ka_ref[:n_off, :], qa)
            s_buf[:n_off, :] = s_off
            m = jnp.maximum(m, jnp.max(s_off, axis=0, keepdims=True))
        return j, m

    def probs(j, m):
        for t in range(j + 1):
            keys = slice(t * tq, (t + 1) * tq)
            p_bufs[j % 2][keys, :] = jnp.exp2(s_bufs[j % 2][keys, :] - m).astype(BF16)
            yield

    def weighted_values(j):
        rows = slice(j * tq, (j + 1) * tq)
        acc = [None, None]
        for t in range(j + 1):
            keys = slice(t * tq, (t + 1) * tq)
            for hh in range(2):
                qs = slice(hh * tq, (hh + 1) * tq)
                part = _dot(vt_ref[hh, :, keys], p_bufs[j % 2][keys, qs])
                acc[hh] = part if acc[hh] is None else acc[hh] + part
            yield
        outs = [a[:FOX_HEAD_DIM] * (1.0 / a[FOX_HEAD_DIM:FOX_HEAD_DIM + 1]) for a in acc]
        o = jnp.concatenate(outs, axis=0).T
        g = g_ref[0, rows, :].astype(F32)
        o_ref[0, rows, :] = (o * g * (1.0 / (1.0 + jnp.exp(-g)))).astype(BF16)
        yield

    stage()
    n_blk = seq // tq
    s_next = scores(0)
    for j in range(n_blk + 1):
        s_cur, s_next = s_next, (scores(j + 1) if j + 1 < n_blk else None)
        streams = []
        if j < n_blk:
            streams.append(probs(*s_cur))
        if j > 0:
            streams.append(weighted_values(j - 1))
        while streams:
            streams = [g for g in streams if next(g, StopIteration) is not StopIteration]


def _fox(proj, fa, fbm):
    bsz, seq, _ = proj.shape
    npair = D_FOX // LANES
    base = 4 * D_HGRN // LANES

    def col(j):
        return pl.BlockSpec((1, seq, LANES), lambda b, h: (b, 0, base + j * npair + h))

    gate = pl.BlockSpec((1, seq, LANES), lambda b, h: (b, 0, 0))
    return pl.pallas_call(
        _fox_kernel,
        grid=(bsz, npair),
        in_specs=[col(0), col(1), col(2), col(3), gate, gate],
        out_specs=pl.BlockSpec((1, seq, LANES), lambda b, h: (b, 0, h)),
        out_shape=jax.ShapeDtypeStruct((bsz, seq, D_FOX), BF16),
        scratch_shapes=[pltpu.VMEM((seq, 2 * LANES), BF16),
                        pltpu.VMEM((2, FOX_HEAD_DIM + FOX_ONES_ROWS, seq), BF16),
                        pltpu.VMEM((seq, 2 * FOX_TQ), F32), pltpu.VMEM((seq, 2 * FOX_TQ), F32),
                        pltpu.VMEM((seq, 2 * FOX_TQ), BF16), pltpu.VMEM((seq, 2 * FOX_TQ), BF16)],
        compiler_params=pltpu.CompilerParams(
            dimension_semantics=("parallel", "parallel"), vmem_limit_bytes=VMEM_LIMIT),
        name="fox",
    )(proj, proj, proj, proj, fa, fbm)


def _outproj_kernel(final, yh_ref, yf_ref, h_ref, p_ref, woh_ref, wof_ref, wp_ref, wg_ref,
                    nw_ref, o_ref, *u_ref):
    h = h_ref[...] + _dot(yh_ref[...], woh_ref[...]) + _dot(yf_ref[...], wof_ref[...])
    ple = _dot(p_ref[...].astype(BF16), wp_ref[...])
    gate = _dot(h.astype(BF16), wg_ref[...])
    h = h + (1.0 / (1.0 + jnp.exp(-gate))) * ple
    if final:
        o_ref[...] = _rmsnorm(h, nw_ref[...])
    else:
        o_ref[...] = h
        u_ref[0][...] = _rmsnorm(h, nw_ref[...]).astype(BF16)


def _outproj(final, yh, yf, h, p, w_out_h, w_out_f, w_ple, w_gate, nw, tm=512):
    m = h.shape[0]

    def rows(width):
        return pl.BlockSpec((tm, width), lambda i: (i, 0))

    def whole(a):
        return pl.BlockSpec(a.shape, lambda i: (0, 0))

    out_specs = [rows(D_MODEL)]
    out_shape = [jax.ShapeDtypeStruct((m, D_MODEL), F32)]
    if not final:
        out_specs.append(rows(D_MODEL))
        out_shape.append(jax.ShapeDtypeStruct((m, D_MODEL), BF16))
    return pl.pallas_call(
        functools.partial(_outproj_kernel, final),
        grid=(m // tm,),
        in_specs=[rows(D_HGRN), rows(D_FOX), rows(D_MODEL), rows(PLE_DIM),
                  whole(w_out_h), whole(w_out_f), whole(w_ple), whole(w_gate), whole(nw)],
        out_specs=out_specs,
        out_shape=out_shape,
        compiler_params=pltpu.CompilerParams(
            dimension_semantics=("parallel",), vmem_limit_bytes=VMEM_LIMIT),
        name="outproj",
    )(yh, yf, h, p, w_out_h, w_out_f, w_ple, w_gate, nw)


def kernel(x, p, norm_w, w_in, fox_fb, hgrn_gn, hgrn_lb_logits, w_out, w_ple, w_ple_gate, final_norm_w):
    bsz, seq, _ = x.shape
    depth = w_in.shape[0]
    m = bsz * seq
    w_main = w_in[:, :, :D_MAIN].astype(BF16)
    w_fgate = jnp.pad(w_in[:, :, D_MAIN:], ((0, 0), (0, 0), (0, LANES - FOX_HEADS))).astype(BF16)
    fb = jnp.pad(fox_fb, ((0, 0), (0, LANES - FOX_HEADS)))
    w_out_b = w_out.astype(BF16)
    w_ple_b = w_ple.astype(BF16)
    w_gate_b = w_ple_gate.astype(BF16)
    fnw = final_norm_w.reshape(1, D_MODEL)

    h = x.reshape(m, D_MODEL)
    u = _prenorm(h, norm_w[0].reshape(1, D_MODEL))
    for i in range(depth):
        final = i == depth - 1
        proj, fz = _inproj(u, w_main[i], w_fgate[i])
        proj = proj.reshape(bsz, seq, D_MAIN)
        fa, fbm = _foxprep(fz.reshape(bsz, seq, LANES), fb[i].reshape(1, LANES))
        yh = _hgrn(i, proj, hgrn_lb_logits, hgrn_gn[i].reshape(1, D_HGRN))
        yf = _fox(proj, fa, fbm)
        outs = _outproj(final, yh.reshape(m, D_HGRN), yf.reshape(m, D_FOX), h,
                        p[i].reshape(m, PLE_DIM), w_out_b[i, :D_HGRN], w_out_b[i, D_HGRN:],
                        w_ple_b[i], w_gate_b[i], fnw if final else norm_w[i + 1].reshape(1, D_MODEL))
        h = outs[0]
        if not final:
            u = outs[1]
    return h.reshape(bsz, seq, D_MODEL)
```

```python
import functools

import jax
import jax.numpy as jnp
from jax import lax
from jax.experimental import pallas as pl
from jax.experimental.pallas import tpu as pltpu

D_MODEL = 1024
PLE_DIM = 256
D_HGRN = 1024
D_FOX = 1024
HGRN_HEAD_DIM = 128
HGRN_HEADS = D_HGRN // HGRN_HEAD_DIM
FOX_HEAD_DIM = 64
FOX_HEADS = D_FOX // FOX_HEAD_DIM
NORM_EPS = 1e-6
D_MAIN = 4 * D_HGRN + 4 * D_FOX

LANES = 128
HGRN_SUB = 32
HGRN_PAIR = 2 * HGRN_SUB
HGRN_SLAB = 128
FOX_TQ = 256
FOX_SLOT = 8
FOX_ONES_ROWS = 16
LOG2E = 1.4426950408889634
HGRN_FAST_MIN_LOG_DECAY = -60.0
VMEM_LIMIT = 48 * 1024 * 1024

F32 = jnp.float32
BF16 = jnp.bfloat16


def _dot(a, b):
    return jnp.dot(a, b, preferred_element_type=F32)


def _dot_nt(a, b):
    return lax.dot_general(a, b, (((1,), (1,)), ((), ())), preferred_element_type=F32)


def _split_bf16(x, n):
    parts = []
    r = x
    for _ in range(n - 1):
        hi = r.astype(BF16)
        parts.append(hi)
        r = r - hi.astype(F32)
    parts.append(r.astype(BF16))
    return parts


def _dot_01_lhs(m01, x, n):
    acc = None
    for part in _split_bf16(x, n):
        t = _dot(m01, part)
        acc = t if acc is None else acc + t
    return acc


def _log_sigmoid(z):
    return jnp.minimum(z, 0.0) - jnp.log1p(jnp.exp(-jnp.abs(z)))


def _rmsnorm(x, w):
    return x * lax.rsqrt(jnp.mean(x * x, axis=-1, keepdims=True) + NORM_EPS) * w


def _prenorm_kernel(x_ref, nw_ref, u_ref):
    u_ref[...] = _rmsnorm(x_ref[...], nw_ref[...]).astype(BF16)


def _prenorm(x, nw, tm=1024):
    m = x.shape[0]
    return pl.pallas_call(
        _prenorm_kernel,
        grid=(m // tm,),
        in_specs=[pl.BlockSpec((tm, D_MODEL), lambda i: (i, 0)), pl.BlockSpec((1, D_MODEL), lambda i: (0, 0))],
        out_specs=pl.BlockSpec((tm, D_MODEL), lambda i: (i, 0)),
        out_shape=jax.ShapeDtypeStruct((m, D_MODEL), BF16),
        compiler_params=pltpu.CompilerParams(
            dimension_semantics=("parallel",), vmem_limit_bytes=VMEM_LIMIT),
        name="prenorm",
    )(x, nw)


def _inproj_kernel(u_ref, w_ref, wz_ref, proj_ref, fz_ref):
    @pl.when(pl.program_id(1) == 0)
    def _():
        fz_ref[...] = _dot(u_ref[...], wz_ref[...])

    acc = _dot(u_ref[...], w_ref[...]).astype(BF16)
    for c in range(proj_ref.shape[0]):
        proj_ref[c] = acc[:, c * LANES:(c + 1) * LANES]


def _inproj(u, w_main, w_gate, tm=1024, tn=2048):
    m = u.shape[0]
    return pl.pallas_call(
        _inproj_kernel,
        grid=(m // tm, D_MAIN // tn),
        in_specs=[
            pl.BlockSpec((tm, D_MODEL), lambda i, j: (i, 0)),
            pl.BlockSpec((D_MODEL, tn), lambda i, j: (0, j)),
            pl.BlockSpec((D_MODEL, LANES), lambda i, j: (0, 0)),
        ],
        out_specs=[
            pl.BlockSpec((tn // LANES, tm, LANES), lambda i, j: (j, i, 0)),
            pl.BlockSpec((tm, LANES), lambda i, j: (i, 0)),
        ],
        out_shape=[
            jax.ShapeDtypeStruct((D_MAIN // LANES, m, LANES), BF16),
            jax.ShapeDtypeStruct((m, LANES), F32),
        ],
        compiler_params=pltpu.CompilerParams(
            dimension_semantics=("parallel", "arbitrary"), vmem_limit_bytes=VMEM_LIMIT),
        name="inproj",
    )(u, w_main, w_gate)


def _hgrn_kernel(layer, q_ref, f_ref, i_ref, g_ref, lbl_ref, gn_ref, o_ref,
                 qf_ref, oi_ref, ds_ref, dec_ref, qs_ref, ks_ref, bs_ref, vs_ref, flag_ref):
    seq = q_ref.shape[1]
    n_slab = seq // HGRN_SLAB
    n_sub = HGRN_SLAB // HGRN_SUB

    if layer > 0:
        logits = lbl_ref[...]
        ex = jnp.exp(logits - jnp.max(logits, axis=0, keepdims=True))
        sm = ex / jnp.sum(ex, axis=0, keepdims=True)
        lb = jnp.sum(sm[1:layer + 1], axis=0, keepdims=True)
        log_lb = jnp.log(lb)
        log_1m_lb = jnp.log1p(-lb)
        one_m_lb = 1.0 - lb

    row = lax.broadcasted_iota(jnp.int32, (HGRN_SLAB, HGRN_SLAB), 0)
    col = lax.broadcasted_iota(jnp.int32, (HGRN_SLAB, HGRN_SLAB), 1)
    causal = (row >= col) & ((row // HGRN_SUB) == (col // HGRN_SUB))
    tril = jnp.where(causal, 1.0, 0.0).astype(BF16)
    row_chunk = lax.broadcasted_iota(jnp.int32, (HGRN_SLAB, HGRN_HEAD_DIM), 0) // HGRN_SUB
    scale = HGRN_HEAD_DIM ** -0.5
    gn = gn_ref[...]

    def slab_rows(si):
        return pl.ds(pl.multiple_of(si * HGRN_SLAB, HGRN_SLAB), HGRN_SLAB)

    def decays(rows):
        fz = f_ref[0, rows, :].astype(F32)
        e = jnp.exp(-jnp.abs(fz))
        r = 1.0 / (1.0 + e)
        log_sig = jnp.minimum(fz, 0.0) + jnp.log(r)
        sig_neg = jnp.where(fz >= 0.0, e * r, r)
        if layer > 0:
            bb = log_1m_lb + log_sig
            log_f = jnp.maximum(log_lb, bb) + jnp.log(1.0 + jnp.exp(-jnp.abs(log_lb - bb)))
            k = one_m_lb * sig_neg
        else:
            log_f = log_sig
            k = sig_neg
        hi, lo = _split_bf16(log_f * LOG2E, 2)
        csum = _dot(tril, jnp.concatenate([hi, lo], axis=1))
        return k, csum[:, :HGRN_HEAD_DIM] + csum[:, HGRN_HEAD_DIM:]

    col_chunk = col // HGRN_SUB
    cross = (row // HGRN_SUB == col_chunk + 1) & (col_chunk % 2 == 0)
    row_pair = lax.broadcasted_iota(jnp.int32, (HGRN_SLAB, HGRN_HEAD_DIM), 0) // HGRN_PAIR
    n_pair = HGRN_SLAB // HGRN_PAIR

    def chunk_end(b, c):
        return b[(c + 1) * HGRN_SUB - 1:(c + 1) * HGRN_SUB, :]

    def per_chunk(vals):
        return jnp.concatenate([jnp.broadcast_to(x, (HGRN_SUB, HGRN_HEAD_DIM)) for x in vals], axis=0)

    def factors(rows, k, b):
        q = q_ref[0, rows, :].astype(F32) * scale
        qf = q * jnp.exp2(b)
        kf = (k * jnp.exp2(-b)).astype(BF16)
        kb = k * jnp.exp2(per_chunk([chunk_end(b, c) for c in range(n_sub)]) - b)
        return qf, kf, kb

    def pass1_a(si):
        return decays(pl.ds(si * HGRN_SLAB, HGRN_SLAB))

    def pass1_b(si, k, b):
        rows = pl.ds(si * HGRN_SLAB, HGRN_SLAB)
        qf, kf, kb = factors(rows, k, b)
        dec = [jnp.exp2(chunk_end(b, c)) for c in range(n_sub)]
        one = jnp.ones((1, HGRN_HEAD_DIM), F32)
        qf_ref[rows, :] = (qf * per_chunk([dec[c - 1] if c % 2 else one for c in range(n_sub)])).astype(BF16)
        kb_pair = (kb * per_chunk([one if c % 2 else dec[c + 1] for c in range(n_sub)])).astype(BF16)
        for pi in range(n_pair):
            dec_ref[pl.ds(si * n_pair + pi, 1), :] = dec[2 * pi] * dec[2 * pi + 1]
        flag_ref[si] = (jnp.min(b) < HGRN_FAST_MIN_LOG_DECAY * LOG2E).astype(jnp.int32)
        qf = qf.astype(BF16)
        return kb_pair, _dot_nt(qf, kf), _dot_nt(qf, kb.astype(BF16))

    def pass1_c(si, kb_pair, s_diag, s_cross):
        rows = pl.ds(si * HGRN_SLAB, HGRN_SLAB)
        v = i_ref[0, rows, :]
        scores = jnp.where(causal, s_diag, jnp.where(cross, s_cross, 0.0))
        oi_ref[rows, :] = _dot(scores.astype(BF16), v)
        v_t = v.astype(F32).T.astype(BF16)
        zero = jnp.zeros((), BF16)
        kb2 = jnp.concatenate([jnp.where(row_pair == pi, kb_pair, zero) for pi in range(n_pair)], axis=1)
        ds2 = _dot(v_t, kb2)
        for pi in range(n_pair):
            ds_ref[si * n_pair + pi] = ds2[:, pi * HGRN_HEAD_DIM:(pi + 1) * HGRN_HEAD_DIM]

    def redo_slab(si, carry):
        @pl.when(flag_ref[si] != 0)
        def _():
            rows = slab_rows(si)
            r0 = pl.multiple_of(si * HGRN_SLAB, HGRN_SLAB)
            k, b = decays(rows)
            qf, _, kb = factors(rows, k, b)
            s_cross = jnp.where(cross, _dot_nt(qf.astype(BF16), kb.astype(BF16)), 0.0)
            oi_ref[rows, :] = _dot(s_cross.astype(BF16), i_ref[0, rows, :])
            qs_ref[...] = q_ref[0, rows, :].astype(F32) * scale
            ks_ref[...] = k
            bs_ref[...] = b
            vs_ref[...] = i_ref[0, rows, :].astype(F32)
            sub_row = lax.broadcasted_iota(jnp.int32, (HGRN_SUB, 1), 0)

            def one_row(t, c):
                c0 = pl.multiple_of((t // HGRN_SUB) * HGRN_SUB, HGRN_SUB)
                crow = pl.ds(c0, HGRN_SUB)
                d = bs_ref[pl.ds(t, 1), :] - bs_ref[crow, :]
                keep = (sub_row + c0) <= t
                w = jnp.where(keep, jnp.exp2(jnp.minimum(d, 0.0)), 0.0)
                sc = jnp.sum(qs_ref[pl.ds(t, 1), :] * ks_ref[crow, :] * w, axis=1, keepdims=True)
                oi_ref[pl.ds(r0 + t, 1), :] += jnp.sum(sc * vs_ref[crow, :], axis=0, keepdims=True)
                return c

            lax.fori_loop(0, HGRN_SLAB, one_row, 0)
        return carry

    def redo():
        lax.fori_loop(0, n_slab, redo_slab, 0)

    def pass2(pi, state_t):
        prow = pl.ds(pi * HGRN_PAIR, HGRN_PAIR)
        o = oi_ref[prow, :] + _dot_nt(qf_ref[prow, :], state_t.astype(BF16))
        o = o * lax.rsqrt(jnp.mean(o * o, axis=-1, keepdims=True) + NORM_EPS) * gn
        g = g_ref[0, prow, :].astype(F32)
        o_ref[0, prow, :] = (o * g * (1.0 / (1.0 + jnp.exp(-g)))).astype(BF16)
        return state_t * dec_ref[pl.ds(pi, 1), :] + ds_ref[pi]

    stage_a = {0: pass1_a(0)}
    stage_b = {}
    for step in range(n_slab + 1):
        if step + 1 < n_slab:
            stage_a[step + 1] = pass1_a(step + 1)
        if step < n_slab:
            stage_b[step] = pass1_b(step, *stage_a.pop(step))
        if step >= 1:
            pass1_c(step - 1, *stage_b.pop(step - 1))
    redo()
    state_t = jnp.zeros((HGRN_HEAD_DIM, HGRN_HEAD_DIM), F32)
    for pi in range(seq // HGRN_PAIR):
        state_t = pass2(pi, state_t)


def _hgrn(layer, proj, bsz, lb_logits, gn):
    _, seq, _ = proj.shape
    depth = lb_logits.shape[0]
    nh = HGRN_HEADS

    def col(j):
        return pl.BlockSpec((1, seq, HGRN_HEAD_DIM), lambda b, h: ((j * nh + h) * bsz + b, 0, 0))

    slab = pltpu.VMEM((HGRN_SLAB, HGRN_HEAD_DIM), F32)
    return pl.pallas_call(
        functools.partial(_hgrn_kernel, layer),
        grid=(bsz, nh),
        in_specs=[col(0), col(1), col(2), col(3),
                  pl.BlockSpec((depth, HGRN_HEAD_DIM), lambda b, h: (0, h)),
                  pl.BlockSpec((1, HGRN_HEAD_DIM), lambda b, h: (0, h))],
        out_specs=pl.BlockSpec((1, seq, HGRN_HEAD_DIM), lambda b, h: (h * bsz + b, 0, 0)),
        out_shape=jax.ShapeDtypeStruct((nh * bsz, seq, HGRN_HEAD_DIM), BF16),
        scratch_shapes=[
            pltpu.VMEM((seq, HGRN_HEAD_DIM), BF16),
            pltpu.VMEM((seq, HGRN_HEAD_DIM), F32),
            pltpu.VMEM((seq // HGRN_PAIR, HGRN_HEAD_DIM, HGRN_HEAD_DIM), F32),
            pltpu.VMEM((seq // HGRN_PAIR, HGRN_HEAD_DIM), F32),
            slab, slab, slab, slab,
            pltpu.SMEM((seq // HGRN_SLAB,), jnp.int32),
        ],
        compiler_params=pltpu.CompilerParams(
            dimension_semantics=("parallel", "parallel"), vmem_limit_bytes=VMEM_LIMIT),
        name="hgrn2",
    )(proj, proj, proj, proj, lb_logits, gn)


def _foxprep_kernel(fz_ref, fb_ref, fa_ref, fbm_ref):
    seq = fz_ref.shape[1]
    blk = LANES
    row = lax.broadcasted_iota(jnp.int32, (blk, blk), 0)
    col = lax.broadcasted_iota(jnp.int32, (blk, blk), 1)
    tril = jnp.where(row >= col, 1.0, 0.0).astype(BF16)
    sel = [jnp.where((col == FOX_SLOT * row + j) & (row < FOX_HEADS), 1.0, 0.0).astype(BF16)
           for j in range(6)]
    lane = lax.broadcasted_iota(jnp.int32, (1, blk), 1) % FOX_SLOT
    ones_a = jnp.where((lane >= 3) & (lane < 6), 1.0, 0.0)
    ones_b = jnp.where(lane < 3, 1.0, 0.0)
    fb = fb_ref[...]
    carry = jnp.zeros((1, blk), F32)
    for i in range(seq // blk):
        rows = slice(i * blk, (i + 1) * blk)
        log_f = _log_sigmoid(fz_ref[0, rows, :] + fb)
        csum = _dot_01_lhs(tril, log_f, 3) + carry
        carry = csum[blk - 1:blk, :]
        parts = _split_bf16(csum * LOG2E, 3)
        fa = ones_a
        fbm = ones_b
        for j in range(3):
            fa = fa + _dot(parts[j], sel[j])
            fbm = fbm - _dot(parts[j], sel[3 + j])
        fa_ref[0, rows, :] = fa.astype(BF16)
        fbm_ref[0, rows, :] = fbm.astype(BF16)


def _foxprep(fz, fb):
    bsz, seq, _ = fz.shape
    spec = pl.BlockSpec((1, seq, LANES), lambda b: (b, 0, 0))
    return pl.pallas_call(
        _foxprep_kernel,
        grid=(bsz,),
        in_specs=[spec, pl.BlockSpec((1, LANES), lambda b: (0, 0))],
        out_specs=[spec, spec],
        out_shape=[jax.ShapeDtypeStruct((bsz, seq, LANES), BF16)] * 2,
        compiler_params=pltpu.CompilerParams(
            dimension_semantics=("parallel",), vmem_limit_bytes=VMEM_LIMIT),
        name="foxprep",
    )(fz, fb)


def _fox_kernel(q_ref, k_ref, v_ref, g_ref, fa_ref, fbm_ref, o_ref, ka_ref, vt_ref,
                s0_ref, s1_ref, p0_ref, p1_ref):
    seq = q_ref.shape[1]
    tq = FOX_TQ
    pair = pl.program_id(1)
    lane = lax.broadcasted_iota(jnp.int32, (1, LANES), 1)
    key_i = lax.broadcasted_iota(jnp.int32, (tq, tq), 0)
    qry_i = lax.broadcasted_iota(jnp.int32, (tq, tq), 1)
    diag_keep = key_i <= qry_i
    keep2 = jnp.concatenate([diag_keep, diag_keep], axis=1)
    neg = -1e30
    qscale = FOX_HEAD_DIM ** -0.5 * LOG2E
    zero = jnp.zeros((), BF16)
    s_bufs = (s0_ref, s1_ref)
    p_bufs = (p0_ref, p1_ref)

    def stage():
        ka_ref[:, :LANES] = k_ref[0]
        ka_ref[:, LANES:] = fbm_ref[0]
        v_t = v_ref[0].astype(F32).T.astype(BF16)
        for hh in range(2):
            vt_ref[hh, :FOX_HEAD_DIM, :] = v_t[hh * FOX_HEAD_DIM:(hh + 1) * FOX_HEAD_DIM]
            vt_ref[hh, FOX_HEAD_DIM:, :] = jnp.ones((FOX_ONES_ROWS, seq), BF16)

    def scores(j):
        rows = slice(j * tq, (j + 1) * tq)
        n_off = j * tq
        q = (q_ref[0, rows, :].astype(F32) * qscale).astype(BF16)
        fa = fa_ref[0, rows, :]
        qa = []
        for hh in range(2):
            hmask = (lane // FOX_HEAD_DIM) == hh
            fmask = (lane // FOX_SLOT) == (2 * pair + hh)
            qa.append(jnp.concatenate(
                [jnp.where(hmask, q, zero), jnp.where(fmask, fa, zero)], axis=1))
        qa = jnp.concatenate(qa, axis=0)
        s_buf = s_bufs[j % 2]
        s_diag = jnp.where(keep2, _dot_nt(ka_ref[rows, :], qa), neg)
        s_buf[rows, :] = s_diag
        m = jnp.max(s_diag, axis=0, keepdims=True)
        if n_off:
            s_off = _dot_nt(ka_ref[:n_off, :], qa)
            s_buf[:n_off, :] = s_off
            m = jnp.maximum(m, jnp.max(s_off, axis=0, keepdims=True))
        return j, m

    def probs(j, m):
        for t in range(j + 1):
            keys = slice(t * tq, (t + 1) * tq)
            p_bufs[j % 2][keys, :] = jnp.exp2(s_bufs[j % 2][keys, :] - m).astype(BF16)
            yield

    def weighted_values(j):
        rows = slice(j * tq, (j + 1) * tq)
        acc = [None, None]
        for t in range(j + 1):
            keys = slice(t * tq, (t + 1) * tq)
            for hh in range(2):
                qs = slice(hh * tq, (hh + 1) * tq)
                part = _dot(vt_ref[hh, :, keys], p_bufs[j % 2][keys, qs])
                acc[hh] = part if acc[hh] is None else acc[hh] + part
            yield
        outs = [a[:FOX_HEAD_DIM] * (1.0 / a[FOX_HEAD_DIM:FOX_HEAD_DIM + 1]) for a in acc]
        o = jnp.concatenate(outs, axis=0).T
        g = g_ref[0, rows, :].astype(F32)
        o_ref[0, rows, :] = (o * g * (1.0 / (1.0 + jnp.exp(-g)))).astype(BF16)
        yield

    stage()
    n_blk = seq // tq
    s_next = scores(0)
    for j in range(n_blk + 1):
        s_cur, s_next = s_next, (scores(j + 1) if j + 1 < n_blk else None)
        streams = []
        if j < n_blk:
            streams.append(probs(*s_cur))
        if j > 0:
            streams.append(weighted_values(j - 1))
        while streams:
            streams = [g for g in streams if next(g, StopIteration) is not StopIteration]


def _fox(proj, bsz, fa, fbm):
    _, seq, _ = proj.shape
    npair = D_FOX // LANES
    base = 4 * D_HGRN // LANES

    def col(j):
        return pl.BlockSpec((1, seq, LANES), lambda b, h: ((base + j * npair + h) * bsz + b, 0, 0))

    gate = pl.BlockSpec((1, seq, LANES), lambda b, h: (b, 0, 0))
    return pl.pallas_call(
        _fox_kernel,
        grid=(bsz, npair),
        in_specs=[col(0), col(1), col(2), col(3), gate, gate],
        out_specs=pl.BlockSpec((1, seq, LANES), lambda b, h: (h * bsz + b, 0, 0)),
        out_shape=jax.ShapeDtypeStruct((npair * bsz, seq, LANES), BF16),
        scratch_shapes=[pltpu.VMEM((seq, 2 * LANES), BF16),
                        pltpu.VMEM((2, FOX_HEAD_DIM + FOX_ONES_ROWS, seq), BF16),
                        pltpu.VMEM((seq, 2 * FOX_TQ), F32), pltpu.VMEM((seq, 2 * FOX_TQ), F32),
                        pltpu.VMEM((seq, 2 * FOX_TQ), BF16), pltpu.VMEM((seq, 2 * FOX_TQ), BF16)],
        compiler_params=pltpu.CompilerParams(
            dimension_semantics=("parallel", "parallel"), vmem_limit_bytes=VMEM_LIMIT),
        name="fox",
    )(proj, proj, proj, proj, fa, fbm)


def _outproj_kernel(final, yh_ref, yf_ref, h_ref, p_ref, woh_ref, wof_ref, wp_ref, wg_ref,
                    nw_ref, o_ref, *u_ref):
    yh = jnp.concatenate([yh_ref[c] for c in range(yh_ref.shape[0])], axis=1)
    yf = jnp.concatenate([yf_ref[c] for c in range(yf_ref.shape[0])], axis=1)
    h = h_ref[...] + _dot(yh, woh_ref[...]) + _dot(yf, wof_ref[...])
    ple = _dot(p_ref[...].astype(BF16), wp_ref[...])
    gate = _dot(h.astype(BF16), wg_ref[...])
    h = h + (1.0 / (1.0 + jnp.exp(-gate))) * ple
    if final:
        o_ref[...] = _rmsnorm(h, nw_ref[...])
    else:
        o_ref[...] = h
        u_ref[0][...] = _rmsnorm(h, nw_ref[...]).astype(BF16)


def _outproj(final, yh, yf, h, p, w_out_h, w_out_f, w_ple, w_gate, nw, tm=512):
    m = h.shape[0]

    def rows(width):
        return pl.BlockSpec((tm, width), lambda i: (i, 0))

    def whole(a):
        return pl.BlockSpec(a.shape, lambda i: (0, 0))

    def blocks(y):
        return pl.BlockSpec((y.shape[0], tm, LANES), lambda i: (0, i, 0))

    out_specs = [rows(D_MODEL)]
    out_shape = [jax.ShapeDtypeStruct((m, D_MODEL), F32)]
    if not final:
        out_specs.append(rows(D_MODEL))
        out_shape.append(jax.ShapeDtypeStruct((m, D_MODEL), BF16))
    return pl.pallas_call(
        functools.partial(_outproj_kernel, final),
        grid=(m // tm,),
        in_specs=[blocks(yh), blocks(yf), rows(D_MODEL), rows(PLE_DIM),
                  whole(w_out_h), whole(w_out_f), whole(w_ple), whole(w_gate), whole(nw)],
        out_specs=out_specs,
        out_shape=out_shape,
        compiler_params=pltpu.CompilerParams(
            dimension_semantics=("parallel",), vmem_limit_bytes=VMEM_LIMIT),
        name="outproj",
    )(yh, yf, h, p, w_out_h, w_out_f, w_ple, w_gate, nw)


def kernel(x, p, norm_w, w_in, fox_fb, hgrn_gn, hgrn_lb_logits, w_out, w_ple, w_ple_gate, final_norm_w):
    bsz, seq, _ = x.shape
    depth = w_in.shape[0]
    m = bsz * seq
    w_main = w_in[:, :, :D_MAIN].astype(BF16)
    w_fgate = jnp.pad(w_in[:, :, D_MAIN:], ((0, 0), (0, 0), (0, LANES - FOX_HEADS))).astype(BF16)
    fb = jnp.pad(fox_fb, ((0, 0), (0, LANES - FOX_HEADS)))
    w_out_b = w_out.astype(BF16)
    w_ple_b = w_ple.astype(BF16)
    w_gate_b = w_ple_gate.astype(BF16)
    fnw = final_norm_w.reshape(1, D_MODEL)

    h = x.reshape(m, D_MODEL)
    u = _prenorm(h, norm_w[0].reshape(1, D_MODEL))
    for i in range(depth):
        final = i == depth - 1
        proj, fz = _inproj(u, w_main[i], w_fgate[i])
        proj = proj.reshape(D_MAIN // LANES * bsz, seq, LANES)
        fa, fbm = _foxprep(fz.reshape(bsz, seq, LANES), fb[i].reshape(1, LANES))
        yh = _hgrn(i, proj, bsz, hgrn_lb_logits, hgrn_gn[i].reshape(1, D_HGRN))
        yf = _fox(proj, bsz, fa, fbm)
        outs = _outproj(final, yh.reshape(D_HGRN // LANES, m, LANES), yf.reshape(D_FOX // LANES, m, LANES), h,
                        p[i].reshape(m, PLE_DIM), w_out_b[i, :D_HGRN], w_out_b[i, D_HGRN:],
                        w_ple_b[i], w_gate_b[i], fnw if final else norm_w[i + 1].reshape(1, D_MODEL))
        h = outs[0]
        if not final:
            u = outs[1]
    return h.reshape(bsz, seq, D_MODEL)
```

```python
import functools

import jax
import jax.numpy as jnp
from jax import lax
from jax.experimental import pallas as pl
from jax.experimental.pallas import tpu as pltpu

D_MODEL = 1024
PLE_DIM = 256
D_HGRN = 1024
D_FOX = 1024
HGRN_HEAD_DIM = 128
HGRN_HEADS = D_HGRN // HGRN_HEAD_DIM
FOX_HEAD_DIM = 64
FOX_HEADS = D_FOX // FOX_HEAD_DIM
NORM_EPS = 1e-6
D_MAIN = 4 * D_HGRN + 4 * D_FOX

LANES = 128
HGRN_SUB = 32
HGRN_SLAB = 128
FOX_TQ = 256
FOX_SLOT = 8
FOX_ONES_ROWS = 16
LOG2E = 1.4426950408889634
HGRN_FAST_MIN_LOG_DECAY = -60.0
VMEM_LIMIT = 48 * 1024 * 1024

F32 = jnp.float32
BF16 = jnp.bfloat16


def _dot(a, b):
    return jnp.dot(a, b, preferred_element_type=F32)


def _dot_nt(a, b):
    return lax.dot_general(a, b, (((1,), (1,)), ((), ())), preferred_element_type=F32)


def _split_bf16(x, n):
    parts = []
    r = x
    for _ in range(n - 1):
        hi = r.astype(BF16)
        parts.append(hi)
        r = r - hi.astype(F32)
    parts.append(r.astype(BF16))
    return parts


def _log_sigmoid(z):
    return jnp.minimum(z, 0.0) - jnp.log1p(jnp.exp(-jnp.abs(z)))


def _rmsnorm(x, w):
    return x * lax.rsqrt(jnp.mean(x * x, axis=-1, keepdims=True) + NORM_EPS) * w


def _inproj_kernel(x_ref, nw_ref, w_ref, wz_ref, proj_ref, fz_ref, u_ref):
    @pl.when(pl.program_id(1) == 0)
    def _():
        u = _rmsnorm(x_ref[...], nw_ref[...]).astype(BF16)
        u_ref[...] = u
        fz_ref[...] = _dot(u, wz_ref[...])

    proj_ref[...] = _dot(u_ref[...], w_ref[...]).astype(BF16)


def _inproj(layer, h, nw, w_main, w_gate, tm=1024, tn=2048):
    m = h.shape[0]
    return pl.pallas_call(
        _inproj_kernel,
        grid=(m // tm, D_MAIN // tn),
        in_specs=[
            pl.BlockSpec((tm, D_MODEL), lambda i, j: (i, 0)),
            pl.BlockSpec((None, 1, D_MODEL), lambda i, j: (layer, 0, 0)),
            pl.BlockSpec((None, D_MODEL, tn), lambda i, j: (layer, 0, j)),
            pl.BlockSpec((None, D_MODEL, LANES), lambda i, j: (layer, 0, 0)),
        ],
        out_specs=[
            pl.BlockSpec((tm, tn), lambda i, j: (i, j)),
            pl.BlockSpec((tm, LANES), lambda i, j: (i, 0)),
        ],
        out_shape=[
            jax.ShapeDtypeStruct((m, D_MAIN), BF16),
            jax.ShapeDtypeStruct((m, LANES), F32),
        ],
        scratch_shapes=[pltpu.VMEM((tm, D_MODEL), BF16)],
        compiler_params=pltpu.CompilerParams(
            dimension_semantics=("parallel", "arbitrary"), vmem_limit_bytes=VMEM_LIMIT),
        name="inproj",
    )(h, nw, w_main, w_gate)


def _hgrn_kernel(layer, q_ref, f_ref, i_ref, g_ref, lbl_ref, gn_ref, o_ref,
                 qf_ref, oi_ref, ds_ref, dec_ref, qs_ref, ks_ref, bs_ref, vs_ref, flag_ref):
    seq = q_ref.shape[1]
    n_slab = seq // HGRN_SLAB
    n_sub = HGRN_SLAB // HGRN_SUB
    n_chunk = seq // HGRN_SUB

    if layer > 0:
        logits = lbl_ref[...]
        ex = jnp.exp(logits - jnp.max(logits, axis=0, keepdims=True))
        sm = ex / jnp.sum(ex, axis=0, keepdims=True)
        lb = jnp.sum(sm[1:layer + 1], axis=0, keepdims=True)
        log_lb = jnp.log(lb)
        log_1m_lb = jnp.log1p(-lb)
        one_m_lb = 1.0 - lb

    row = lax.broadcasted_iota(jnp.int32, (HGRN_SLAB, HGRN_SLAB), 0)
    col = lax.broadcasted_iota(jnp.int32, (HGRN_SLAB, HGRN_SLAB), 1)
    causal = (row >= col) & ((row // HGRN_SUB) == (col // HGRN_SUB))
    tril = jnp.where(causal, 1.0, 0.0).astype(BF16)
    row_chunk = lax.broadcasted_iota(jnp.int32, (HGRN_SLAB, HGRN_HEAD_DIM), 0) // HGRN_SUB
    scale = HGRN_HEAD_DIM ** -0.5
    gn = gn_ref[...]

    def slab_rows(si):
        return pl.ds(pl.multiple_of(si * HGRN_SLAB, HGRN_SLAB), HGRN_SLAB)

    def decays(rows):
        fz = f_ref[0, rows, :].astype(F32)
        e = jnp.exp(-jnp.abs(fz))
        r = 1.0 / (1.0 + e)
        log_sig = jnp.minimum(fz, 0.0) + jnp.log(r)
        sig_neg = jnp.where(fz >= 0.0, e * r, r)
        if layer > 0:
            bb = log_1m_lb + log_sig
            log_f = jnp.maximum(log_lb, bb) + jnp.log(1.0 + jnp.exp(-jnp.abs(log_lb - bb)))
            k = one_m_lb * sig_neg
        else:
            log_f = log_sig
            k = sig_neg
        hi, lo = _split_bf16(log_f * LOG2E, 2)
        csum = _dot(tril, jnp.concatenate([hi, lo], axis=1))
        return k, csum[:, :HGRN_HEAD_DIM] + csum[:, HGRN_HEAD_DIM:]

    def pass1_a(si):
        return decays(pl.ds(si * HGRN_SLAB, HGRN_SLAB))

    def pass1_b(si, k, b):
        rows = pl.ds(si * HGRN_SLAB, HGRN_SLAB)
        b_last = jnp.concatenate(
            [jnp.broadcast_to(b[(c + 1) * HGRN_SUB - 1:(c + 1) * HGRN_SUB, :], (HGRN_SUB, HGRN_HEAD_DIM))
             for c in range(n_sub)], axis=0)
        q = q_ref[0, rows, :].astype(F32) * scale
        qf = (q * jnp.exp2(b)).astype(BF16)
        kf = (k * jnp.exp2(-b)).astype(BF16)
        kb = (k * jnp.exp2(b_last - b)).astype(BF16)
        qf_ref[rows, :] = qf
        return b, kb, _dot_nt(qf, kf)

    def pass1_c(si, b, kb, scores):
        rows = pl.ds(si * HGRN_SLAB, HGRN_SLAB)
        v = i_ref[0, rows, :]
        oi_ref[rows, :] = _dot(jnp.where(causal, scores, 0.0).astype(BF16), v)
        v_t = v.astype(F32).T.astype(BF16)
        zero = jnp.zeros((), BF16)
        for c in range(0, n_sub, 2):
            kb2 = jnp.concatenate([jnp.where(row_chunk == c, kb, zero),
                                   jnp.where(row_chunk == c + 1, kb, zero)], axis=1)
            ds2 = _dot(v_t, kb2)
            ds_ref[si * n_sub + c] = ds2[:, :HGRN_HEAD_DIM]
            ds_ref[si * n_sub + c + 1] = ds2[:, HGRN_HEAD_DIM:]
        for c in range(n_sub):
            ci = si * n_sub + c
            dec_ref[pl.ds(ci, 1), :] = jnp.exp2(b[(c + 1) * HGRN_SUB - 1:(c + 1) * HGRN_SUB, :])
        flag_ref[si] = (jnp.min(b) < HGRN_FAST_MIN_LOG_DECAY * LOG2E).astype(jnp.int32)

    def redo_slab(si, carry):
        @pl.when(flag_ref[si] != 0)
        def _():
            rows = slab_rows(si)
            r0 = pl.multiple_of(si * HGRN_SLAB, HGRN_SLAB)
            k, b = decays(rows)
            qs_ref[...] = q_ref[0, rows, :].astype(F32) * scale
            ks_ref[...] = k
            bs_ref[...] = b
            vs_ref[...] = i_ref[0, rows, :].astype(F32)
            sub_row = lax.broadcasted_iota(jnp.int32, (HGRN_SUB, 1), 0)

            def one_row(t, c):
                c0 = pl.multiple_of((t // HGRN_SUB) * HGRN_SUB, HGRN_SUB)
                crow = pl.ds(c0, HGRN_SUB)
                d = bs_ref[pl.ds(t, 1), :] - bs_ref[crow, :]
                keep = (sub_row + c0) <= t
                w = jnp.where(keep, jnp.exp2(jnp.minimum(d, 0.0)), 0.0)
                sc = jnp.sum(qs_ref[pl.ds(t, 1), :] * ks_ref[crow, :] * w, axis=1, keepdims=True)
                oi_ref[pl.ds(r0 + t, 1), :] = jnp.sum(sc * vs_ref[crow, :], axis=0, keepdims=True)
                return c

            lax.fori_loop(0, HGRN_SLAB, one_row, 0)
        return carry

    def redo():
        lax.fori_loop(0, n_slab, redo_slab, 0)

    def pass2(ci, state_t):
        crow = pl.ds(ci * HGRN_SUB, HGRN_SUB)
        o = oi_ref[crow, :] + _dot_nt(qf_ref[crow, :], state_t.astype(BF16))
        o = o * lax.rsqrt(jnp.mean(o * o, axis=-1, keepdims=True) + NORM_EPS) * gn
        g = g_ref[0, crow, :].astype(F32)
        o_ref[0, crow, :] = (o * g * (1.0 / (1.0 + jnp.exp(-g)))).astype(BF16)
        return state_t * dec_ref[pl.ds(ci, 1), :] + ds_ref[ci]

    stage_a = {0: pass1_a(0)}
    stage_b = {}
    for step in range(n_slab + 1):
        if step + 1 < n_slab:
            stage_a[step + 1] = pass1_a(step + 1)
        if step < n_slab:
            stage_b[step] = pass1_b(step, *stage_a.pop(step))
        if step >= 1:
            pass1_c(step - 1, *stage_b.pop(step - 1))
    redo()
    state_t = jnp.zeros((HGRN_HEAD_DIM, HGRN_HEAD_DIM), F32)
    for ci in range(n_chunk):
        state_t = pass2(ci, state_t)


def _hgrn(layer, proj, lb_logits, gn):
    bsz, seq, _ = proj.shape
    depth = lb_logits.shape[0]
    nh = HGRN_HEADS

    def col(j):
        return pl.BlockSpec((1, seq, HGRN_HEAD_DIM), lambda b, h: (b, 0, j * nh + h))

    slab = pltpu.VMEM((HGRN_SLAB, HGRN_HEAD_DIM), F32)
    return pl.pallas_call(
        functools.partial(_hgrn_kernel, layer),
        grid=(bsz, nh),
        in_specs=[col(0), col(1), col(2), col(3),
                  pl.BlockSpec((depth, HGRN_HEAD_DIM), lambda b, h: (0, h)),
                  pl.BlockSpec((None, 1, HGRN_HEAD_DIM), lambda b, h: (layer, 0, h))],
        out_specs=pl.BlockSpec((1, seq, HGRN_HEAD_DIM), lambda b, h: (b, 0, h)),
        out_shape=jax.ShapeDtypeStruct((bsz, seq, D_HGRN), BF16),
        scratch_shapes=[
            pltpu.VMEM((seq, HGRN_HEAD_DIM), BF16),
            pltpu.VMEM((seq, HGRN_HEAD_DIM), F32),
            pltpu.VMEM((seq // HGRN_SUB, HGRN_HEAD_DIM, HGRN_HEAD_DIM), F32),
            pltpu.VMEM((seq // HGRN_SUB, HGRN_HEAD_DIM), F32),
            slab, slab, slab, slab,
            pltpu.SMEM((seq // HGRN_SLAB,), jnp.int32),
        ],
        compiler_params=pltpu.CompilerParams(
            dimension_semantics=("parallel", "parallel"), vmem_limit_bytes=VMEM_LIMIT),
        name="hgrn2",
    )(proj, proj, proj, proj, lb_logits, gn)


def _foxprep_kernel(fz_ref, fb_ref, fa_ref, fbm_ref):
    seq = fz_ref.shape[1]
    blk = LANES
    n_blk = seq // blk
    row = lax.broadcasted_iota(jnp.int32, (blk, blk), 0)
    col = lax.broadcasted_iota(jnp.int32, (blk, blk), 1)
    tril = jnp.where(row >= col, 1.0, 0.0).astype(BF16)

    def place(j, sign):
        return jnp.where((col == FOX_SLOT * row + j) & (row < FOX_HEADS), sign, 0.0).astype(BF16)

    sel = jnp.concatenate(
        [jnp.concatenate([place(j, 1.0), place(3 + j, -1.0)], axis=1) for j in range(3)], axis=0)
    lane = lax.broadcasted_iota(jnp.int32, (1, blk), 1) % FOX_SLOT
    ones = jnp.concatenate([jnp.where((lane >= 3) & (lane < 6), 1.0, 0.0), jnp.where(lane < 3, 1.0, 0.0)], axis=1)
    fb = fb_ref[...]

    def prefix(i):
        log_f = _log_sigmoid(fz_ref[0, i * blk:(i + 1) * blk, :] + fb)
        t = _dot(tril, jnp.concatenate(_split_bf16(log_f, 3), axis=1))
        return t[:, :blk] + t[:, blk:2 * blk] + t[:, 2 * blk:]

    local = [prefix(i) for i in range(n_blk)]
    carry = jnp.zeros((1, blk), F32)
    for i in range(n_blk):
        rows = slice(i * blk, (i + 1) * blk)
        csum = local[i] + carry
        carry = carry + local[i][blk - 1:blk, :]
        packed = ones + _dot(jnp.concatenate(_split_bf16(csum * LOG2E, 3), axis=1), sel)
        fa_ref[0, rows, :] = packed[:, :blk].astype(BF16)
        fbm_ref[0, rows, :] = packed[:, blk:].astype(BF16)


def _foxprep(layer, fz, fb):
    bsz, seq, _ = fz.shape
    spec = pl.BlockSpec((1, seq, LANES), lambda b: (b, 0, 0))
    return pl.pallas_call(
        _foxprep_kernel,
        grid=(bsz,),
        in_specs=[spec, pl.BlockSpec((None, 1, LANES), lambda b: (layer, 0, 0))],
        out_specs=[spec, spec],
        out_shape=[jax.ShapeDtypeStruct((bsz, seq, LANES), BF16)] * 2,
        compiler_params=pltpu.CompilerParams(
            dimension_semantics=("parallel",), vmem_limit_bytes=VMEM_LIMIT),
        name="foxprep",
    )(fz, fb)


def _fox_kernel(q_ref, k_ref, v_ref, g_ref, fa_ref, fbm_ref, o_ref, ka_ref, vt_ref,
                s0_ref, s1_ref, p0_ref, p1_ref):
    seq = q_ref.shape[1]
    tq = FOX_TQ
    pair = pl.program_id(1)
    lane = lax.broadcasted_iota(jnp.int32, (1, LANES), 1)
    key_i = lax.broadcasted_iota(jnp.int32, (tq, tq), 0)
    qry_i = lax.broadcasted_iota(jnp.int32, (tq, tq), 1)
    diag_keep = key_i <= qry_i
    keep2 = jnp.concatenate([diag_keep, diag_keep], axis=1)
    neg = -1e30
    qscale = FOX_HEAD_DIM ** -0.5 * LOG2E
    zero = jnp.zeros((), BF16)
    s_bufs = (s0_ref, s1_ref)
    p_bufs = (p0_ref, p1_ref)

    def stage():
        ka_ref[:, :LANES] = k_ref[0]
        ka_ref[:, LANES:] = fbm_ref[0]
        v_t = v_ref[0].astype(F32).T.astype(BF16)
        for hh in range(2):
            vt_ref[hh, :FOX_HEAD_DIM, :] = v_t[hh * FOX_HEAD_DIM:(hh + 1) * FOX_HEAD_DIM]
            vt_ref[hh, FOX_HEAD_DIM:, :] = jnp.ones((FOX_ONES_ROWS, seq), BF16)

    def scores(j):
        rows = slice(j * tq, (j + 1) * tq)
        n_off = j * tq
        q = (q_ref[0, rows, :].astype(F32) * qscale).astype(BF16)
        fa = fa_ref[0, rows, :]
        qa = []
        for hh in range(2):
            hmask = (lane // FOX_HEAD_DIM) == hh
            fmask = (lane // FOX_SLOT) == (2 * pair + hh)
            qa.append(jnp.concatenate(
                [jnp.where(hmask, q, zero), jnp.where(fmask, fa, zero)], axis=1))
        qa = jnp.concatenate(qa, axis=0)
        s_buf = s_bufs[j % 2]
        s_diag = jnp.where(keep2, _dot_nt(ka_ref[rows, :], qa), neg)
        s_buf[rows, :] = s_diag
        m = jnp.max(s_diag, axis=0, keepdims=True)
        if n_off:
            s_off = _dot_nt(ka_ref[:n_off, :], qa)
            s_buf[:n_off, :] = s_off
            m = jnp.maximum(m, jnp.max(s_off, axis=0, keepdims=True))
        return j, m

    def probs(j, m):
        for t in range(j + 1):
            keys = slice(t * tq, (t + 1) * tq)
            p_bufs[j % 2][keys, :] = jnp.exp2(s_bufs[j % 2][keys, :] - m).astype(BF16)
            yield

    def weighted_values(j):
        rows = slice(j * tq, (j + 1) * tq)
        acc = [None, None]
        for t in range(j + 1):
            keys = slice(t * tq, (t + 1) * tq)
            for hh in range(2):
                qs = slice(hh * tq, (hh + 1) * tq)
                part = _dot(vt_ref[hh, :, keys], p_bufs[j % 2][keys, qs])
                acc[hh] = part if acc[hh] is None else acc[hh] + part
            yield
        outs = [a[:FOX_HEAD_DIM] * (1.0 / a[FOX_HEAD_DIM:FOX_HEAD_DIM + 1]) for a in acc]
        o = jnp.concatenate(outs, axis=0).T
        g = g_ref[0, rows, :].astype(F32)
        o_ref[0, rows, :] = (o * g * (1.0 / (1.0 + jnp.exp(-g)))).astype(BF16)
        yield

    stage()
    n_blk = seq // tq
    s_next = scores(0)
    for j in range(n_blk + 1):
        s_cur, s_next = s_next, (scores(j + 1) if j + 1 < n_blk else None)
        streams = []
        if j < n_blk:
            streams.append(probs(*s_cur))
        if j > 0:
            streams.append(weighted_values(j - 1))
        while streams:
            streams = [g for g in streams if next(g, StopIteration) is not StopIteration]


def _fox(proj, fa, fbm):
    bsz, seq, _ = proj.shape
    npair = D_FOX // LANES
    base = 4 * D_HGRN // LANES

    def col(j):
        return pl.BlockSpec((1, seq, LANES), lambda b, h: (b, 0, base + j * npair + h))

    gate = pl.BlockSpec((1, seq, LANES), lambda b, h: (b, 0, 0))
    return pl.pallas_call(
        _fox_kernel,
        grid=(bsz, npair),
        in_specs=[col(0), col(1), col(2), col(3), gate, gate],
        out_specs=pl.BlockSpec((1, seq, LANES), lambda b, h: (b, 0, h)),
        out_shape=jax.ShapeDtypeStruct((bsz, seq, D_FOX), BF16),
        scratch_shapes=[pltpu.VMEM((seq, 2 * LANES), BF16),
                        pltpu.VMEM((2, FOX_HEAD_DIM + FOX_ONES_ROWS, seq), BF16),
                        pltpu.VMEM((seq, 2 * FOX_TQ), F32), pltpu.VMEM((seq, 2 * FOX_TQ), F32),
                        pltpu.VMEM((seq, 2 * FOX_TQ), BF16), pltpu.VMEM((seq, 2 * FOX_TQ), BF16)],
        compiler_params=pltpu.CompilerParams(
            dimension_semantics=("parallel", "parallel"), vmem_limit_bytes=VMEM_LIMIT),
        name="fox",
    )(proj, proj, proj, proj, fa, fbm)


def _outproj_kernel(final, yh_ref, yf_ref, h_ref, p_ref, woh_ref, wof_ref, wp_ref, wg_ref,
                    fnw_ref, o_ref):
    h = h_ref[...] + _dot(yh_ref[...], woh_ref[...]) + _dot(yf_ref[...], wof_ref[...])
    ple = _dot(p_ref[...].astype(BF16), wp_ref[...])
    gate = _dot(h.astype(BF16), wg_ref[...])
    h = h + (1.0 / (1.0 + jnp.exp(-gate))) * ple
    if final:
        h = _rmsnorm(h, fnw_ref[...])
    o_ref[...] = h


def _outproj(layer, final, yh, yf, h, p, w_out, w_ple, w_gate, fnw, tm=512):
    m = h.shape[0]

    def rows(width):
        return pl.BlockSpec((tm, width), lambda i: (i, 0))

    def weight(k, blk=0):
        return pl.BlockSpec((None, k, D_MODEL), lambda i: (layer, blk, 0))

    return pl.pallas_call(
        functools.partial(_outproj_kernel, final),
        grid=(m // tm,),
        in_specs=[rows(D_HGRN), rows(D_FOX), rows(D_MODEL),
                  pl.BlockSpec((None, tm, PLE_DIM), lambda i: (layer, i, 0)),
                  weight(D_HGRN, 0), weight(D_FOX, 1), weight(PLE_DIM), weight(D_MODEL),
                  pl.BlockSpec((1, D_MODEL), lambda i: (0, 0))],
        out_specs=rows(D_MODEL),
        out_shape=jax.ShapeDtypeStruct((m, D_MODEL), F32),
        compiler_params=pltpu.CompilerParams(
            dimension_semantics=("parallel",), vmem_limit_bytes=VMEM_LIMIT),
        name="outproj",
    )(yh, yf, h, p, w_out, w_out, w_ple, w_gate, fnw)


def kernel(x, p, norm_w, w_in, fox_fb, hgrn_gn, hgrn_lb_logits, w_out, w_ple, w_ple_gate, final_norm_w):
    bsz, seq, _ = x.shape
    depth = w_in.shape[0]
    m = bsz * seq
    w_main = w_in[:, :, :D_MAIN].astype(BF16)
    w_fgate = jnp.pad(w_in[:, :, D_MAIN:], ((0, 0), (0, 0), (0, LANES - FOX_HEADS))).astype(BF16)
    fb = jnp.pad(fox_fb, ((0, 0), (0, LANES - FOX_HEADS)))
    w_out_b = w_out.astype(BF16)
    w_ple_b = w_ple.astype(BF16)
    w_gate_b = w_ple_gate.astype(BF16)
    fnw = final_norm_w.reshape(1, D_MODEL)

    p = p.reshape(depth, m, PLE_DIM)
    h = x.reshape(m, D_MODEL)
    for i in range(depth):
        proj, fz = _inproj(i, h, norm_w.reshape(depth, 1, D_MODEL), w_main, w_fgate)
        proj = proj.reshape(bsz, seq, D_MAIN)
        yh = _hgrn(i, proj, hgrn_lb_logits, hgrn_gn.reshape(depth, 1, D_HGRN))
        fa, fbm = _foxprep(i, fz.reshape(bsz, seq, LANES), fb.reshape(depth, 1, LANES))
        yf = _fox(proj, fa, fbm)
        h = _outproj(i, i == depth - 1, yh.reshape(m, D_HGRN), yf.reshape(m, D_FOX), h,
                     p, w_out_b, w_ple_b, w_gate_b, fnw)
    return h.reshape(bsz, seq, D_MODEL)
```

```python
import functools

import jax
import jax.numpy as jnp
from jax import lax
from jax.experimental import pallas as pl
from jax.experimental.pallas import tpu as pltpu

D_MODEL = 1024
PLE_DIM = 256
D_HGRN = 1024
D_FOX = 1024
HGRN_HEAD_DIM = 128
HGRN_HEADS = D_HGRN // HGRN_HEAD_DIM
FOX_HEAD_DIM = 64
FOX_HEADS = D_FOX // FOX_HEAD_DIM
NORM_EPS = 1e-6
D_MAIN = 4 * D_HGRN + 4 * D_FOX

LANES = 128
HGRN_SUB = 32
HGRN_SLAB = 128
FOX_TQ = 256
FOX_SLOT = 8
FOX_ONES_ROWS = 16
LOG2E = 1.4426950408889634
HGRN_FAST_MIN_LOG_DECAY = -60.0
VMEM_LIMIT = 48 * 1024 * 1024

F32 = jnp.float32
BF16 = jnp.bfloat16


def _dot(a, b):
    return jnp.dot(a, b, preferred_element_type=F32)


def _dot_nt(a, b):
    return lax.dot_general(a, b, (((1,), (1,)), ((), ())), preferred_element_type=F32)


def _split_bf16(x, n):
    parts = []
    r = x
    for _ in range(n - 1):
        hi = r.astype(BF16)
        parts.append(hi)
        r = r - hi.astype(F32)
    parts.append(r.astype(BF16))
    return parts


def _log_sigmoid(z):
    return jnp.minimum(z, 0.0) - jnp.log1p(jnp.exp(-jnp.abs(z)))


def _rmsnorm(x, w):
    return x * lax.rsqrt(jnp.mean(x * x, axis=-1, keepdims=True) + NORM_EPS) * w


def _inproj_kernel(x_ref, nw_ref, w_ref, wz_ref, proj_ref, fz_ref, u_ref):
    @pl.when(pl.program_id(1) == 0)
    def _():
        u = _rmsnorm(x_ref[...], nw_ref[...]).astype(BF16)
        u_ref[...] = u
        fz_ref[...] = _dot(u, wz_ref[...])

    proj_ref[...] = _dot(u_ref[...], w_ref[...]).astype(BF16)


def _inproj(layer, h, nw, w_main, w_gate, tm=1024, tn=4096):
    m = h.shape[0]
    return pl.pallas_call(
        _inproj_kernel,
        grid=(m // tm, D_MAIN // tn),
        in_specs=[
            pl.BlockSpec((tm, D_MODEL), lambda i, j: (i, 0)),
            pl.BlockSpec((None, 1, D_MODEL), lambda i, j: (layer, 0, 0)),
            pl.BlockSpec((None, D_MODEL, tn), lambda i, j: (layer, 0, j)),
            pl.BlockSpec((None, D_MODEL, LANES), lambda i, j: (layer, 0, 0)),
        ],
        out_specs=[
            pl.BlockSpec((tm, tn), lambda i, j: (i, j)),
            pl.BlockSpec((tm, LANES), lambda i, j: (i, 0)),
        ],
        out_shape=[
            jax.ShapeDtypeStruct((m, D_MAIN), BF16),
            jax.ShapeDtypeStruct((m, LANES), F32),
        ],
        scratch_shapes=[pltpu.VMEM((tm, D_MODEL), BF16)],
        compiler_params=pltpu.CompilerParams(
            dimension_semantics=("parallel", "arbitrary"), vmem_limit_bytes=VMEM_LIMIT),
        name="inproj",
    )(h, nw, w_main, w_gate)


def _hgrn_kernel(layer, q_ref, f_ref, i_ref, g_ref, lbl_ref, gn_ref, o_ref,
                 qf_ref, oi_ref, ds_ref, dec_ref, qs_ref, ks_ref, bs_ref, vs_ref, flag_ref):
    seq = q_ref.shape[1]
    n_slab = seq // HGRN_SLAB
    n_sub = HGRN_SLAB // HGRN_SUB
    n_chunk = seq // HGRN_SUB

    if layer > 0:
        logits = lbl_ref[...]
        ex = jnp.exp(logits - jnp.max(logits, axis=0, keepdims=True))
        sm = ex / jnp.sum(ex, axis=0, keepdims=True)
        lb = jnp.sum(sm[1:layer + 1], axis=0, keepdims=True)
        log_lb = jnp.log(lb)
        log_1m_lb = jnp.log1p(-lb)
        one_m_lb = 1.0 - lb

    row = lax.broadcasted_iota(jnp.int32, (HGRN_SLAB, HGRN_SLAB), 0)
    col = lax.broadcasted_iota(jnp.int32, (HGRN_SLAB, HGRN_SLAB), 1)
    causal = (row >= col) & ((row // HGRN_SUB) == (col // HGRN_SUB))
    tril = jnp.where(causal, 1.0, 0.0).astype(BF16)
    row_chunk = lax.broadcasted_iota(jnp.int32, (HGRN_SLAB, HGRN_HEAD_DIM), 0) // HGRN_SUB
    scale = HGRN_HEAD_DIM ** -0.5
    gn = gn_ref[...]

    def slab_rows(si):
        return pl.ds(pl.multiple_of(si * HGRN_SLAB, HGRN_SLAB), HGRN_SLAB)

    def decays(rows):
        fz = f_ref[0, rows, :].astype(F32)
        e = jnp.exp(-jnp.abs(fz))
        r = 1.0 / (1.0 + e)
        log_sig = jnp.minimum(fz, 0.0) + jnp.log(r)
        sig_neg = jnp.where(fz >= 0.0, e * r, r)
        if layer > 0:
            bb = log_1m_lb + log_sig
            log_f = jnp.maximum(log_lb, bb) + jnp.log(1.0 + jnp.exp(-jnp.abs(log_lb - bb)))
            k = one_m_lb * sig_neg
        else:
            log_f = log_sig
            k = sig_neg
        hi, lo = _split_bf16(log_f * LOG2E, 2)
        csum = _dot(tril, jnp.concatenate([hi, lo], axis=1))
        return k, csum[:, :HGRN_HEAD_DIM] + csum[:, HGRN_HEAD_DIM:]

    def pass1_a(si):
        return decays(pl.ds(si * HGRN_SLAB, HGRN_SLAB))

    def pass1_b(si, k, b):
        rows = pl.ds(si * HGRN_SLAB, HGRN_SLAB)
        b_last = jnp.concatenate(
            [jnp.broadcast_to(b[(c + 1) * HGRN_SUB - 1:(c + 1) * HGRN_SUB, :], (HGRN_SUB, HGRN_HEAD_DIM))
             for c in range(n_sub)], axis=0)
        q = q_ref[0, rows, :].astype(F32) * scale
        qf = (q * jnp.exp2(b)).astype(BF16)
        kf = (k * jnp.exp2(-b)).astype(BF16)
        kb = (k * jnp.exp2(b_last - b)).astype(BF16)
        qf_ref[rows, :] = qf
        return b, kb, _dot_nt(qf, kf)

    def pass1_c(si, b, kb, scores):
        rows = pl.ds(si * HGRN_SLAB, HGRN_SLAB)
        v = i_ref[0, rows, :]
        oi_ref[rows, :] = _dot(jnp.where(causal, scores, 0.0).astype(BF16), v)
        v_t = v.astype(F32).T.astype(BF16)
        zero = jnp.zeros((), BF16)
        for c in range(0, n_sub, 2):
            kb2 = jnp.concatenate([jnp.where(row_chunk == c, kb, zero),
                                   jnp.where(row_chunk == c + 1, kb, zero)], axis=1)
            ds2 = _dot(v_t, kb2)
            ds_ref[si * n_sub + c] = ds2[:, :HGRN_HEAD_DIM]
            ds_ref[si * n_sub + c + 1] = ds2[:, HGRN_HEAD_DIM:]
        for c in range(n_sub):
            ci = si * n_sub + c
            dec_ref[pl.ds(ci, 1), :] = jnp.exp2(b[(c + 1) * HGRN_SUB - 1:(c + 1) * HGRN_SUB, :])
        flag_ref[si] = (jnp.min(b) < HGRN_FAST_MIN_LOG_DECAY * LOG2E).astype(jnp.int32)

    def redo_slab(si, carry):
        @pl.when(flag_ref[si] != 0)
        def _():
            rows = slab_rows(si)
            r0 = pl.multiple_of(si * HGRN_SLAB, HGRN_SLAB)
            k, b = decays(rows)
            qs_ref[...] = q_ref[0, rows, :].astype(F32) * scale
            ks_ref[...] = k
            bs_ref[...] = b
            vs_ref[...] = i_ref[0, rows, :].astype(F32)
            sub_row = lax.broadcasted_iota(jnp.int32, (HGRN_SUB, 1), 0)

            def one_row(t, c):
                c0 = pl.multiple_of((t // HGRN_SUB) * HGRN_SUB, HGRN_SUB)
                crow = pl.ds(c0, HGRN_SUB)
                d = bs_ref[pl.ds(t, 1), :] - bs_ref[crow, :]
                keep = (sub_row + c0) <= t
                w = jnp.where(keep, jnp.exp2(jnp.minimum(d, 0.0)), 0.0)
                sc = jnp.sum(qs_ref[pl.ds(t, 1), :] * ks_ref[crow, :] * w, axis=1, keepdims=True)
                oi_ref[pl.ds(r0 + t, 1), :] = jnp.sum(sc * vs_ref[crow, :], axis=0, keepdims=True)
                return c

            lax.fori_loop(0, HGRN_SLAB, one_row, 0)
        return carry

    def redo():
        lax.fori_loop(0, n_slab, redo_slab, 0)

    def pass2(ci, state_t):
        crow = pl.ds(ci * HGRN_SUB, HGRN_SUB)
        o = oi_ref[crow, :] + _dot_nt(qf_ref[crow, :], state_t.astype(BF16))
        o = o * lax.rsqrt(jnp.mean(o * o, axis=-1, keepdims=True) + NORM_EPS) * gn
        g = g_ref[0, crow, :].astype(F32)
        o_ref[0, crow, :] = (o * g * (1.0 / (1.0 + jnp.exp(-g)))).astype(BF16)
        return state_t * dec_ref[pl.ds(ci, 1), :] + ds_ref[ci]

    stage_a = {0: pass1_a(0)}
    stage_b = {}
    for step in range(n_slab + 1):
        if step + 1 < n_slab:
            stage_a[step + 1] = pass1_a(step + 1)
        if step < n_slab:
            stage_b[step] = pass1_b(step, *stage_a.pop(step))
        if step >= 1:
            pass1_c(step - 1, *stage_b.pop(step - 1))
    redo()
    state_t = jnp.zeros((HGRN_HEAD_DIM, HGRN_HEAD_DIM), F32)
    for ci in range(n_chunk):
        state_t = pass2(ci, state_t)


def _hgrn(layer, proj, lb_logits, gn):
    bsz, seq, _ = proj.shape
    depth = lb_logits.shape[0]
    nh = HGRN_HEADS

    def col(j):
        return pl.BlockSpec((1, seq, HGRN_HEAD_DIM), lambda b, h: (b, 0, j * nh + h))

    slab = pltpu.VMEM((HGRN_SLAB, HGRN_HEAD_DIM), F32)
    return pl.pallas_call(
        functools.partial(_hgrn_kernel, layer),
        grid=(bsz, nh),
        in_specs=[col(0), col(1), col(2), col(3),
                  pl.BlockSpec((depth, HGRN_HEAD_DIM), lambda b, h: (0, h)),
                  pl.BlockSpec((None, 1, HGRN_HEAD_DIM), lambda b, h: (layer, 0, h))],
        out_specs=pl.BlockSpec((1, seq, HGRN_HEAD_DIM), lambda b, h: (b, 0, h)),
        out_shape=jax.ShapeDtypeStruct((bsz, seq, D_HGRN), BF16),
        scratch_shapes=[
            pltpu.VMEM((seq, HGRN_HEAD_DIM), BF16),
            pltpu.VMEM((seq, HGRN_HEAD_DIM), F32),
            pltpu.VMEM((seq // HGRN_SUB, HGRN_HEAD_DIM, HGRN_HEAD_DIM), F32),
            pltpu.VMEM((seq // HGRN_SUB, HGRN_HEAD_DIM), F32),
            slab, slab, slab, slab,
            pltpu.SMEM((seq // HGRN_SLAB,), jnp.int32),
        ],
        compiler_params=pltpu.CompilerParams(
            dimension_semantics=("parallel", "parallel"), vmem_limit_bytes=VMEM_LIMIT),
        name="hgrn2",
    )(proj, proj, proj, proj, lb_logits, gn)


def _foxprep_kernel(fz_ref, fb_ref, fa_ref, fbm_ref):
    seq = fz_ref.shape[1]
    blk = LANES
    n_blk = seq // blk
    row = lax.broadcasted_iota(jnp.int32, (blk, blk), 0)
    col = lax.broadcasted_iota(jnp.int32, (blk, blk), 1)
    tril = jnp.where(row >= col, 1.0, 0.0).astype(BF16)

    def place(j, sign):
        return jnp.where((col == FOX_SLOT * row + j) & (row < FOX_HEADS), sign, 0.0).astype(BF16)

    sel = jnp.concatenate(
        [jnp.concatenate([place(j, 1.0), place(3 + j, -1.0)], axis=1) for j in range(3)], axis=0)
    lane = lax.broadcasted_iota(jnp.int32, (1, blk), 1) % FOX_SLOT
    ones = jnp.concatenate([jnp.where((lane >= 3) & (lane < 6), 1.0, 0.0), jnp.where(lane < 3, 1.0, 0.0)], axis=1)
    fb = fb_ref[...]

    def prefix(i):
        log_f = _log_sigmoid(fz_ref[0, i * blk:(i + 1) * blk, :] + fb)
        t = _dot(tril, jnp.concatenate(_split_bf16(log_f, 3), axis=1))
        return t[:, :blk] + t[:, blk:2 * blk] + t[:, 2 * blk:]

    local = [prefix(i) for i in range(n_blk)]
    carry = jnp.zeros((1, blk), F32)
    for i in range(n_blk):
        rows = slice(i * blk, (i + 1) * blk)
        csum = local[i] + carry
        carry = carry + local[i][blk - 1:blk, :]
        packed = ones + _dot(jnp.concatenate(_split_bf16(csum * LOG2E, 3), axis=1), sel)
        fa_ref[0, rows, :] = packed[:, :blk].astype(BF16)
        fbm_ref[0, rows, :] = packed[:, blk:].astype(BF16)


def _foxprep(layer, fz, fb):
    bsz, seq, _ = fz.shape
    spec = pl.BlockSpec((1, seq, LANES), lambda b: (b, 0, 0))
    return pl.pallas_call(
        _foxprep_kernel,
        grid=(bsz,),
        in_specs=[spec, pl.BlockSpec((None, 1, LANES), lambda b: (layer, 0, 0))],
        out_specs=[spec, spec],
        out_shape=[jax.ShapeDtypeStruct((bsz, seq, LANES), BF16)] * 2,
        compiler_params=pltpu.CompilerParams(
            dimension_semantics=("parallel",), vmem_limit_bytes=VMEM_LIMIT),
        name="foxprep",
    )(fz, fb)


def _fox_kernel(q_ref, k_ref, v_ref, g_ref, fa_ref, fbm_ref, o_ref, ka_ref, vt_ref,
                s0_ref, s1_ref, p0_ref, p1_ref):
    seq = q_ref.shape[1]
    tq = FOX_TQ
    pair = pl.program_id(1)
    lane = lax.broadcasted_iota(jnp.int32, (1, LANES), 1)
    key_i = lax.broadcasted_iota(jnp.int32, (tq, tq), 0)
    qry_i = lax.broadcasted_iota(jnp.int32, (tq, tq), 1)
    diag_keep = key_i <= qry_i
    keep2 = jnp.concatenate([diag_keep, diag_keep], axis=1)
    neg = -1e30
    qscale = FOX_HEAD_DIM ** -0.5 * LOG2E
    zero = jnp.zeros((), BF16)
    s_bufs = (s0_ref, s1_ref)
    p_bufs = (p0_ref, p1_ref)

    def stage():
        ka_ref[:, :LANES] = k_ref[0]
        ka_ref[:, LANES:] = fbm_ref[0]
        v_t = v_ref[0].astype(F32).T.astype(BF16)
        for hh in range(2):
            vt_ref[hh, :FOX_HEAD_DIM, :] = v_t[hh * FOX_HEAD_DIM:(hh + 1) * FOX_HEAD_DIM]
            vt_ref[hh, FOX_HEAD_DIM:, :] = jnp.ones((FOX_ONES_ROWS, seq), BF16)

    def scores(j):
        rows = slice(j * tq, (j + 1) * tq)
        n_off = j * tq
        q = (q_ref[0, rows, :].astype(F32) * qscale).astype(BF16)
        fa = fa_ref[0, rows, :]
        qa = []
        for hh in range(2):
            hmask = (lane // FOX_HEAD_DIM) == hh
            fmask = (lane // FOX_SLOT) == (2 * pair + hh)
            qa.append(jnp.concatenate(
                [jnp.where(hmask, q, zero), jnp.where(fmask, fa, zero)], axis=1))
        qa = jnp.concatenate(qa, axis=0)
        s_buf = s_bufs[j % 2]
        s_diag = jnp.where(keep2, _dot_nt(ka_ref[rows, :], qa), neg)
        s_buf[rows, :] = s_diag
        m = jnp.max(s_diag, axis=0, keepdims=True)
        if n_off:
            s_off = _dot_nt(ka_ref[:n_off, :], qa)
            s_buf[:n_off, :] = s_off
            m = jnp.maximum(m, jnp.max(s_off, axis=0, keepdims=True))
        return j, m

    def probs(j, m):
        for t in range(j + 1):
            keys = slice(t * tq, (t + 1) * tq)
            p_bufs[j % 2][keys, :] = jnp.exp2(s_bufs[j % 2][keys, :] - m).astype(BF16)
            yield

    def weighted_values(j):
        rows = slice(j * tq, (j + 1) * tq)
        acc = [None, None]
        for t in range(j + 1):
            keys = slice(t * tq, (t + 1) * tq)
            for hh in range(2):
                qs = slice(hh * tq, (hh + 1) * tq)
                part = _dot(vt_ref[hh, :, keys], p_bufs[j % 2][keys, qs])
                acc[hh] = part if acc[hh] is None else acc[hh] + part
            yield
        outs = [a[:FOX_HEAD_DIM] * (1.0 / a[FOX_HEAD_DIM:FOX_HEAD_DIM + 1]) for a in acc]
        o = jnp.concatenate(outs, axis=0).T
        g = g_ref[0, rows, :].astype(F32)
        o_ref[0, rows, :] = (o * g * (1.0 / (1.0 + jnp.exp(-g)))).astype(BF16)
        yield

    stage()
    n_blk = seq // tq
    s_next = scores(0)
    for j in range(n_blk + 1):
        s_cur, s_next = s_next, (scores(j + 1) if j + 1 < n_blk else None)
        streams = []
        if j < n_blk:
            streams.append(probs(*s_cur))
        if j > 0:
            streams.append(weighted_values(j - 1))
        while streams:
            streams = [g for g in streams if next(g, StopIteration) is not StopIteration]


def _fox(proj, fa, fbm):
    bsz, seq, _ = proj.shape
    npair = D_FOX // LANES
    base = 4 * D_HGRN // LANES

    def col(j):
        return pl.BlockSpec((1, seq, LANES), lambda b, h: (b, 0, base + j * npair + h))

    gate = pl.BlockSpec((1, seq, LANES), lambda b, h: (b, 0, 0))
    return pl.pallas_call(
        _fox_kernel,
        grid=(bsz, npair),
        in_specs=[col(0), col(1), col(2), col(3), gate, gate],
        out_specs=pl.BlockSpec((1, seq, LANES), lambda b, h: (b, 0, h)),
        out_shape=jax.ShapeDtypeStruct((bsz, seq, D_FOX), BF16),
        scratch_shapes=[pltpu.VMEM((seq, 2 * LANES), BF16),
                        pltpu.VMEM((2, FOX_HEAD_DIM + FOX_ONES_ROWS, seq), BF16),
                        pltpu.VMEM((seq, 2 * FOX_TQ), F32), pltpu.VMEM((seq, 2 * FOX_TQ), F32),
                        pltpu.VMEM((seq, 2 * FOX_TQ), BF16), pltpu.VMEM((seq, 2 * FOX_TQ), BF16)],
        compiler_params=pltpu.CompilerParams(
            dimension_semantics=("parallel", "parallel"), vmem_limit_bytes=VMEM_LIMIT),
        name="fox",
    )(proj, proj, proj, proj, fa, fbm)


def _outproj_kernel(final, yh_ref, yf_ref, h_ref, p_ref, woh_ref, wof_ref, wp_ref, wg_ref,
                    fnw_ref, o_ref):
    h = h_ref[...] + _dot(yh_ref[...], woh_ref[...]) + _dot(yf_ref[...], wof_ref[...])
    ple = _dot(p_ref[...].astype(BF16), wp_ref[...])
    gate = _dot(h.astype(BF16), wg_ref[...])
    h = h + (1.0 / (1.0 + jnp.exp(-gate))) * ple
    if final:
        h = _rmsnorm(h, fnw_ref[...])
    o_ref[...] = h


def _outproj(layer, final, yh, yf, h, p, w_out, w_ple, w_gate, fnw, tm=1024):
    m = h.shape[0]

    def rows(width):
        return pl.BlockSpec((tm, width), lambda i: (i, 0))

    def weight(k, blk=0):
        return pl.BlockSpec((None, k, D_MODEL), lambda i: (layer, blk, 0))

    return pl.pallas_call(
        functools.partial(_outproj_kernel, final),
        grid=(m // tm,),
        in_specs=[rows(D_HGRN), rows(D_FOX), rows(D_MODEL),
                  pl.BlockSpec((None, tm, PLE_DIM), lambda i: (layer, i, 0)),
                  weight(D_HGRN, 0), weight(D_FOX, 1), weight(PLE_DIM), weight(D_MODEL),
                  pl.BlockSpec((1, D_MODEL), lambda i: (0, 0))],
        out_specs=rows(D_MODEL),
        out_shape=jax.ShapeDtypeStruct((m, D_MODEL), F32),
        compiler_params=pltpu.CompilerParams(
            dimension_semantics=("parallel",), vmem_limit_bytes=VMEM_LIMIT),
        name="outproj",
    )(yh, yf, h, p, w_out, w_out, w_ple, w_gate, fnw)


def kernel(x, p, norm_w, w_in, fox_fb, hgrn_gn, hgrn_lb_logits, w_out, w_ple, w_ple_gate, final_norm_w):
    bsz, seq, _ = x.shape
    depth = w_in.shape[0]
    m = bsz * seq
    w_main = w_in.astype(BF16)
    w_fgate = jnp.pad(w_in[:, :, D_MAIN:], ((0, 0), (0, 0), (0, LANES - FOX_HEADS))).astype(BF16)
    fb = jnp.pad(fox_fb, ((0, 0), (0, LANES - FOX_HEADS)))
    w_out_b = w_out.astype(BF16)
    w_ple_b = w_ple.astype(BF16)
    w_gate_b = w_ple_gate.astype(BF16)
    fnw = final_norm_w.reshape(1, D_MODEL)

    p = p.reshape(depth, m, PLE_DIM)
    h = x.reshape(m, D_MODEL)
    for i in range(depth):
        proj, fz = _inproj(i, h, norm_w.reshape(depth, 1, D_MODEL), w_main, w_fgate)
        proj = proj.reshape(bsz, seq, D_MAIN)
        yh = _hgrn(i, proj, hgrn_lb_logits, hgrn_gn.reshape(depth, 1, D_HGRN))
        fa, fbm = _foxprep(i, fz.reshape(bsz, seq, LANES), fb.reshape(depth, 1, LANES))
        yf = _fox(proj, fa, fbm)
        h = _outproj(i, i == depth - 1, yh.reshape(m, D_HGRN), yf.reshape(m, D_FOX), h,
                     p, w_out_b, w_ple_b, w_gate_b, fnw)
    return h.reshape(bsz, seq, D_MODEL)
```

```python
import functools

import jax
import jax.numpy as jnp
from jax import lax
from jax.experimental import pallas as pl
from jax.experimental.pallas import tpu as pltpu

D_MODEL = 1024
PLE_DIM = 256
D_HGRN = 1024
D_FOX = 1024
HGRN_HEAD_DIM = 128
HGRN_HEADS = D_HGRN // HGRN_HEAD_DIM
FOX_HEAD_DIM = 64
FOX_HEADS = D_FOX // FOX_HEAD_DIM
NORM_EPS = 1e-6
D_MAIN = 4 * D_HGRN + 4 * D_FOX

LANES = 128
HGRN_SUB = 32
HGRN_SLAB = 128
FOX_TQ = 256
MIXER_BATCH_ROWS = 4
FOX_SLOT = 8
FOX_ONES_ROWS = 16
LOG2E = 1.4426950408889634
HGRN_FAST_MIN_LOG_DECAY = -60.0
VMEM_LIMIT = 48 * 1024 * 1024

F32 = jnp.float32
BF16 = jnp.bfloat16


def _dot(a, b):
    return jnp.dot(a, b, preferred_element_type=F32)


def _dot_nt(a, b):
    return lax.dot_general(a, b, (((1,), (1,)), ((), ())), preferred_element_type=F32)


def _split_bf16(x, n):
    parts = []
    r = x
    for _ in range(n - 1):
        hi = r.astype(BF16)
        parts.append(hi)
        r = r - hi.astype(F32)
    parts.append(r.astype(BF16))
    return parts


def _log_sigmoid(z):
    return jnp.minimum(z, 0.0) - jnp.log1p(jnp.exp(-jnp.abs(z)))


def _rmsnorm(x, w):
    return x * lax.rsqrt(jnp.mean(x * x, axis=-1, keepdims=True) + NORM_EPS) * w


def _inproj_kernel(x_ref, nw_ref, w_ref, wz_ref, proj_ref, fz_ref, u_ref):
    @pl.when(pl.program_id(1) == 0)
    def _():
        u = _rmsnorm(x_ref[...], nw_ref[...]).astype(BF16)
        u_ref[...] = u
        fz_ref[...] = _dot(u, wz_ref[...])

    proj_ref[...] = _dot(u_ref[...], w_ref[...]).astype(BF16)


def _inproj(layer, h, nw, w_main, w_gate, tm=1024, tn=4096):
    m = h.shape[0]
    return pl.pallas_call(
        _inproj_kernel,
        grid=(m // tm, D_MAIN // tn),
        in_specs=[
            pl.BlockSpec((tm, D_MODEL), lambda i, j: (i, 0)),
            pl.BlockSpec((None, 1, D_MODEL), lambda i, j: (layer, 0, 0)),
            pl.BlockSpec((None, D_MODEL, tn), lambda i, j: (layer, 0, j)),
            pl.BlockSpec((None, D_MODEL, LANES), lambda i, j: (layer, 0, 0)),
        ],
        out_specs=[
            pl.BlockSpec((tm, tn), lambda i, j: (i, j)),
            pl.BlockSpec((tm, LANES), lambda i, j: (i, 0)),
        ],
        out_shape=[
            jax.ShapeDtypeStruct((m, D_MAIN), BF16),
            jax.ShapeDtypeStruct((m, LANES), F32),
        ],
        scratch_shapes=[pltpu.VMEM((tm, D_MODEL), BF16)],
        compiler_params=pltpu.CompilerParams(
            dimension_semantics=("parallel", "arbitrary"), vmem_limit_bytes=VMEM_LIMIT),
        name="inproj",
    )(h, nw, w_main, w_gate)


def _hgrn_kernel(layer, q_ref, *refs):
    def one_batch_row(bi, carry):
        _hgrn_row(layer, bi, q_ref, *refs)
        return carry

    lax.fori_loop(0, q_ref.shape[0], one_batch_row, 0)


def _hgrn_row(layer, bi, q_ref, f_ref, i_ref, g_ref, lbl_ref, gn_ref, o_ref,
              qf_ref, oi_ref, ds_ref, dec_ref, qs_ref, ks_ref, bs_ref, vs_ref, flag_ref):
    seq = q_ref.shape[1]
    n_slab = seq // HGRN_SLAB
    n_sub = HGRN_SLAB // HGRN_SUB
    n_chunk = seq // HGRN_SUB

    if layer > 0:
        logits = lbl_ref[...]
        ex = jnp.exp(logits - jnp.max(logits, axis=0, keepdims=True))
        sm = ex / jnp.sum(ex, axis=0, keepdims=True)
        lb = jnp.sum(sm[1:layer + 1], axis=0, keepdims=True)
        log_lb = jnp.log(lb)
        log_1m_lb = jnp.log1p(-lb)
        one_m_lb = 1.0 - lb

    row = lax.broadcasted_iota(jnp.int32, (HGRN_SLAB, HGRN_SLAB), 0)
    col = lax.broadcasted_iota(jnp.int32, (HGRN_SLAB, HGRN_SLAB), 1)
    causal = (row >= col) & ((row // HGRN_SUB) == (col // HGRN_SUB))
    tril = jnp.where(causal, 1.0, 0.0).astype(BF16)
    row_chunk = lax.broadcasted_iota(jnp.int32, (HGRN_SLAB, HGRN_HEAD_DIM), 0) // HGRN_SUB
    scale = HGRN_HEAD_DIM ** -0.5
    gn = gn_ref[...]

    def slab_rows(si):
        return pl.ds(pl.multiple_of(si * HGRN_SLAB, HGRN_SLAB), HGRN_SLAB)

    def decays(rows):
        fz = f_ref[bi, rows, :].astype(F32)
        e = jnp.exp(-jnp.abs(fz))
        r = 1.0 / (1.0 + e)
        log_sig = jnp.minimum(fz, 0.0) + jnp.log(r)
        sig_neg = jnp.where(fz >= 0.0, e * r, r)
        if layer > 0:
            bb = log_1m_lb + log_sig
            log_f = jnp.maximum(log_lb, bb) + jnp.log(1.0 + jnp.exp(-jnp.abs(log_lb - bb)))
            k = one_m_lb * sig_neg
        else:
            log_f = log_sig
            k = sig_neg
        hi, lo = _split_bf16(log_f * LOG2E, 2)
        csum = _dot(tril, jnp.concatenate([hi, lo], axis=1))
        return k, csum[:, :HGRN_HEAD_DIM] + csum[:, HGRN_HEAD_DIM:]

    def pass1_a(si):
        return decays(pl.ds(si * HGRN_SLAB, HGRN_SLAB))

    def pass1_b(si, k, b):
        rows = pl.ds(si * HGRN_SLAB, HGRN_SLAB)
        b_last = jnp.concatenate(
            [jnp.broadcast_to(b[(c + 1) * HGRN_SUB - 1:(c + 1) * HGRN_SUB, :], (HGRN_SUB, HGRN_HEAD_DIM))
             for c in range(n_sub)], axis=0)
        q = q_ref[bi, rows, :].astype(F32) * scale
        qf = (q * jnp.exp2(b)).astype(BF16)
        kf = (k * jnp.exp2(-b)).astype(BF16)
        kb = (k * jnp.exp2(b_last - b)).astype(BF16)
        qf_ref[rows, :] = qf
        return b, kb, _dot_nt(qf, kf)

    def pass1_c(si, b, kb, scores):
        rows = pl.ds(si * HGRN_SLAB, HGRN_SLAB)
        v = i_ref[bi, rows, :]
        oi_ref[rows, :] = _dot(jnp.where(causal, scores, 0.0).astype(BF16), v)
        v_t = v.astype(F32).T.astype(BF16)
        zero = jnp.zeros((), BF16)
        for c in range(0, n_sub, 2):
            kb2 = jnp.concatenate([jnp.where(row_chunk == c, kb, zero),
                                   jnp.where(row_chunk == c + 1, kb, zero)], axis=1)
            ds2 = _dot(v_t, kb2)
            ds_ref[si * n_sub + c] = ds2[:, :HGRN_HEAD_DIM]
            ds_ref[si * n_sub + c + 1] = ds2[:, HGRN_HEAD_DIM:]
        for c in range(n_sub):
            ci = si * n_sub + c
            dec_ref[pl.ds(ci, 1), :] = jnp.exp2(b[(c + 1) * HGRN_SUB - 1:(c + 1) * HGRN_SUB, :])
        flag_ref[si] = (jnp.min(b) < HGRN_FAST_MIN_LOG_DECAY * LOG2E).astype(jnp.int32)

    def redo_slab(si, carry):
        @pl.when(flag_ref[si] != 0)
        def _():
            rows = slab_rows(si)
            r0 = pl.multiple_of(si * HGRN_SLAB, HGRN_SLAB)
            k, b = decays(rows)
            qs_ref[...] = q_ref[bi, rows, :].astype(F32) * scale
            ks_ref[...] = k
            bs_ref[...] = b
            vs_ref[...] = i_ref[bi, rows, :].astype(F32)
            sub_row = lax.broadcasted_iota(jnp.int32, (HGRN_SUB, 1), 0)

            def one_row(t, c):
                c0 = pl.multiple_of((t // HGRN_SUB) * HGRN_SUB, HGRN_SUB)
                crow = pl.ds(c0, HGRN_SUB)
                d = bs_ref[pl.ds(t, 1), :] - bs_ref[crow, :]
                keep = (sub_row + c0) <= t
                w = jnp.where(keep, jnp.exp2(jnp.minimum(d, 0.0)), 0.0)
                sc = jnp.sum(qs_ref[pl.ds(t, 1), :] * ks_ref[crow, :] * w, axis=1, keepdims=True)
                oi_ref[pl.ds(r0 + t, 1), :] = jnp.sum(sc * vs_ref[crow, :], axis=0, keepdims=True)
                return c

            lax.fori_loop(0, HGRN_SLAB, one_row, 0)
        return carry

    def redo():
        lax.fori_loop(0, n_slab, redo_slab, 0)

    def pass2(ci, state_t):
        crow = pl.ds(ci * HGRN_SUB, HGRN_SUB)
        o = oi_ref[crow, :] + _dot_nt(qf_ref[crow, :], state_t.astype(BF16))
        o = o * lax.rsqrt(jnp.mean(o * o, axis=-1, keepdims=True) + NORM_EPS) * gn
        g = g_ref[bi, crow, :].astype(F32)
        o_ref[bi, crow, :] = (o * g * (1.0 / (1.0 + jnp.exp(-g)))).astype(BF16)
        return state_t * dec_ref[pl.ds(ci, 1), :] + ds_ref[ci]

    stage_a = {0: pass1_a(0)}
    stage_b = {}
    for step in range(n_slab + 1):
        if step + 1 < n_slab:
            stage_a[step + 1] = pass1_a(step + 1)
        if step < n_slab:
            stage_b[step] = pass1_b(step, *stage_a.pop(step))
        if step >= 1:
            pass1_c(step - 1, *stage_b.pop(step - 1))
    redo()
    state_t = jnp.zeros((HGRN_HEAD_DIM, HGRN_HEAD_DIM), F32)
    for ci in range(n_chunk):
        state_t = pass2(ci, state_t)


def _hgrn(layer, proj, lb_logits, gn):
    bsz, seq, _ = proj.shape
    depth = lb_logits.shape[0]
    nh = HGRN_HEADS

    nb = MIXER_BATCH_ROWS
    assert bsz % nb == 0, (bsz, nb)

    def col(j):
        return pl.BlockSpec((nb, seq, HGRN_HEAD_DIM), lambda b, h: (b, 0, j * nh + h))

    slab = pltpu.VMEM((HGRN_SLAB, HGRN_HEAD_DIM), F32)
    return pl.pallas_call(
        functools.partial(_hgrn_kernel, layer),
        grid=(bsz // nb, nh),
        in_specs=[col(0), col(1), col(2), col(3),
                  pl.BlockSpec((depth, HGRN_HEAD_DIM), lambda b, h: (0, h)),
                  pl.BlockSpec((None, 1, HGRN_HEAD_DIM), lambda b, h: (layer, 0, h))],
        out_specs=pl.BlockSpec((nb, seq, HGRN_HEAD_DIM), lambda b, h: (b, 0, h)),
        out_shape=jax.ShapeDtypeStruct((bsz, seq, D_HGRN), BF16),
        scratch_shapes=[
            pltpu.VMEM((seq, HGRN_HEAD_DIM), BF16),
            pltpu.VMEM((seq, HGRN_HEAD_DIM), F32),
            pltpu.VMEM((seq // HGRN_SUB, HGRN_HEAD_DIM, HGRN_HEAD_DIM), F32),
            pltpu.VMEM((seq // HGRN_SUB, HGRN_HEAD_DIM), F32),
            slab, slab, slab, slab,
            pltpu.SMEM((seq // HGRN_SLAB,), jnp.int32),
        ],
        compiler_params=pltpu.CompilerParams(
            dimension_semantics=("parallel", "parallel"), vmem_limit_bytes=VMEM_LIMIT),
        name="hgrn2",
    )(proj, proj, proj, proj, lb_logits, gn)


def _foxprep_kernel(fz_ref, fb_ref, fa_ref, fbm_ref):
    seq = fz_ref.shape[1]
    blk = LANES
    n_blk = seq // blk
    row = lax.broadcasted_iota(jnp.int32, (blk, blk), 0)
    col = lax.broadcasted_iota(jnp.int32, (blk, blk), 1)
    tril = jnp.where(row >= col, 1.0, 0.0).astype(BF16)

    def place(j, sign):
        return jnp.where((col == FOX_SLOT * row + j) & (row < FOX_HEADS), sign, 0.0).astype(BF16)

    sel = jnp.concatenate(
        [jnp.concatenate([place(j, 1.0), place(3 + j, -1.0)], axis=1) for j in range(3)], axis=0)
    lane = lax.broadcasted_iota(jnp.int32, (1, blk), 1) % FOX_SLOT
    ones = jnp.concatenate([jnp.where((lane >= 3) & (lane < 6), 1.0, 0.0), jnp.where(lane < 3, 1.0, 0.0)], axis=1)
    fb = fb_ref[...]

    def prefix(i):
        log_f = _log_sigmoid(fz_ref[0, i * blk:(i + 1) * blk, :] + fb)
        t = _dot(tril, jnp.concatenate(_split_bf16(log_f, 3), axis=1))
        return t[:, :blk] + t[:, blk:2 * blk] + t[:, 2 * blk:]

    local = [prefix(i) for i in range(n_blk)]
    carry = jnp.zeros((1, blk), F32)
    for i in range(n_blk):
        rows = slice(i * blk, (i + 1) * blk)
        csum = local[i] + carry
        carry = carry + local[i][blk - 1:blk, :]
        packed = ones + _dot(jnp.concatenate(_split_bf16(csum * LOG2E, 3), axis=1), sel)
        fa_ref[0, rows, :] = packed[:, :blk].astype(BF16)
        fbm_ref[0, rows, :] = packed[:, blk:].astype(BF16)


def _foxprep(layer, fz, fb):
    bsz, seq, _ = fz.shape
    spec = pl.BlockSpec((1, seq, LANES), lambda b: (b, 0, 0))
    return pl.pallas_call(
        _foxprep_kernel,
        grid=(bsz,),
        in_specs=[spec, pl.BlockSpec((None, 1, LANES), lambda b: (layer, 0, 0))],
        out_specs=[spec, spec],
        out_shape=[jax.ShapeDtypeStruct((bsz, seq, LANES), BF16)] * 2,
        compiler_params=pltpu.CompilerParams(
            dimension_semantics=("parallel",), vmem_limit_bytes=VMEM_LIMIT),
        name="foxprep",
    )(fz, fb)


def _fox_kernel(q_ref, *refs):
    def one_batch_row(bi, carry):
        _fox_row(bi, q_ref, *refs)
        return carry

    lax.fori_loop(0, q_ref.shape[0], one_batch_row, 0)


def _fox_row(bi, q_ref, k_ref, v_ref, g_ref, fa_ref, fbm_ref, o_ref, ka_ref, vt_ref,
             s0_ref, s1_ref, p0_ref, p1_ref):
    seq = q_ref.shape[1]
    tq = FOX_TQ
    pair = pl.program_id(1)
    lane = lax.broadcasted_iota(jnp.int32, (1, LANES), 1)
    key_i = lax.broadcasted_iota(jnp.int32, (tq, tq), 0)
    qry_i = lax.broadcasted_iota(jnp.int32, (tq, tq), 1)
    diag_keep = key_i <= qry_i
    keep2 = jnp.concatenate([diag_keep, diag_keep], axis=1)
    neg = -1e30
    qscale = FOX_HEAD_DIM ** -0.5 * LOG2E
    zero = jnp.zeros((), BF16)
    s_bufs = (s0_ref, s1_ref)
    p_bufs = (p0_ref, p1_ref)

    def stage():
        ka_ref[:, :LANES] = k_ref[bi]
        ka_ref[:, LANES:] = fbm_ref[bi]
        v_t = v_ref[bi].astype(F32).T.astype(BF16)
        for hh in range(2):
            vt_ref[hh, :FOX_HEAD_DIM, :] = v_t[hh * FOX_HEAD_DIM:(hh + 1) * FOX_HEAD_DIM]
            vt_ref[hh, FOX_HEAD_DIM:, :] = jnp.ones((FOX_ONES_ROWS, seq), BF16)

    def scores(j):
        rows = slice(j * tq, (j + 1) * tq)
        n_off = j * tq
        q = (q_ref[bi, rows, :].astype(F32) * qscale).astype(BF16)
        fa = fa_ref[bi, rows, :]
        qa = []
        for hh in range(2):
            hmask = (lane // FOX_HEAD_DIM) == hh
            fmask = (lane // FOX_SLOT) == (2 * pair + hh)
            qa.append(jnp.concatenate(
                [jnp.where(hmask, q, zero), jnp.where(fmask, fa, zero)], axis=1))
        qa = jnp.concatenate(qa, axis=0)
        s_buf = s_bufs[j % 2]
        s_diag = jnp.where(keep2, _dot_nt(ka_ref[rows, :], qa), neg)
        s_buf[rows, :] = s_diag
        m = jnp.max(s_diag, axis=0, keepdims=True)
        if n_off:
            s_off = _dot_nt(ka_ref[:n_off, :], qa)
            s_buf[:n_off, :] = s_off
            m = jnp.maximum(m, jnp.max(s_off, axis=0, keepdims=True))
        return j, m

    def probs(j, m):
        for t in range(j + 1):
            keys = slice(t * tq, (t + 1) * tq)
            p_bufs[j % 2][keys, :] = jnp.exp2(s_bufs[j % 2][keys, :] - m).astype(BF16)
            yield

    def weighted_values(j):
        rows = slice(j * tq, (j + 1) * tq)
        acc = [None, None]
        for t in range(j + 1):
            keys = slice(t * tq, (t + 1) * tq)
            for hh in range(2):
                qs = slice(hh * tq, (hh + 1) * tq)
                part = _dot(vt_ref[hh, :, keys], p_bufs[j % 2][keys, qs])
                acc[hh] = part if acc[hh] is None else acc[hh] + part
            yield
        outs = [a[:FOX_HEAD_DIM] * (1.0 / a[FOX_HEAD_DIM:FOX_HEAD_DIM + 1]) for a in acc]
        o = jnp.concatenate(outs, axis=0).T
        g = g_ref[bi, rows, :].astype(F32)
        o_ref[bi, rows, :] = (o * g * (1.0 / (1.0 + jnp.exp(-g)))).astype(BF16)
        yield

    stage()
    n_blk = seq // tq
    s_next = scores(0)
    for j in range(n_blk + 1):
        s_cur, s_next = s_next, (scores(j + 1) if j + 1 < n_blk else None)
        streams = []
        if j < n_blk:
            streams.append(probs(*s_cur))
        if j > 0:
            streams.append(weighted_values(j - 1))
        while streams:
            streams = [g for g in streams if next(g, StopIteration) is not StopIteration]


def _fox(proj, fa, fbm):
    bsz, seq, _ = proj.shape
    npair = D_FOX // LANES
    base = 4 * D_HGRN // LANES

    nb = MIXER_BATCH_ROWS
    assert bsz % nb == 0, (bsz, nb)

    def col(j):
        return pl.BlockSpec((nb, seq, LANES), lambda b, h: (b, 0, base + j * npair + h))

    gate = pl.BlockSpec((nb, seq, LANES), lambda b, h: (b, 0, 0))
    return pl.pallas_call(
        _fox_kernel,
        grid=(bsz // nb, npair),
        in_specs=[col(0), col(1), col(2), col(3), gate, gate],
        out_specs=pl.BlockSpec((nb, seq, LANES), lambda b, h: (b, 0, h)),
        out_shape=jax.ShapeDtypeStruct((bsz, seq, D_FOX), BF16),
        scratch_shapes=[pltpu.VMEM((seq, 2 * LANES), BF16),
                        pltpu.VMEM((2, FOX_HEAD_DIM + FOX_ONES_ROWS, seq), BF16),
                        pltpu.VMEM((seq, 2 * FOX_TQ), F32), pltpu.VMEM((seq, 2 * FOX_TQ), F32),
                        pltpu.VMEM((seq, 2 * FOX_TQ), BF16), pltpu.VMEM((seq, 2 * FOX_TQ), BF16)],
        compiler_params=pltpu.CompilerParams(
            dimension_semantics=("parallel", "parallel"), vmem_limit_bytes=VMEM_LIMIT),
        name="fox",
    )(proj, proj, proj, proj, fa, fbm)


def _outproj_kernel(final, yh_ref, yf_ref, h_ref, p_ref, woh_ref, wof_ref, wp_ref, wg_ref,
                    fnw_ref, o_ref):
    h = h_ref[...] + _dot(yh_ref[...], woh_ref[...]) + _dot(yf_ref[...], wof_ref[...])
    ple = _dot(p_ref[...].astype(BF16), wp_ref[...])
    gate = _dot(h.astype(BF16), wg_ref[...])
    h = h + (1.0 / (1.0 + jnp.exp(-gate))) * ple
    if final:
        h = _rmsnorm(h, fnw_ref[...])
    o_ref[...] = h


def _outproj(layer, final, yh, yf, h, p, w_out, w_ple, w_gate, fnw, tm=1024):
    m = h.shape[0]

    def rows(width):
        return pl.BlockSpec((tm, width), lambda i: (i, 0))

    def weight(k, blk=0):
        return pl.BlockSpec((None, k, D_MODEL), lambda i: (layer, blk, 0))

    return pl.pallas_call(
        functools.partial(_outproj_kernel, final),
        grid=(m // tm,),
        in_specs=[rows(D_HGRN), rows(D_FOX), rows(D_MODEL),
                  pl.BlockSpec((None, tm, PLE_DIM), lambda i: (layer, i, 0)),
                  weight(D_HGRN, 0), weight(D_FOX, 1), weight(PLE_DIM), weight(D_MODEL),
                  pl.BlockSpec((1, D_MODEL), lambda i: (0, 0))],
        out_specs=rows(D_MODEL),
        out_shape=jax.ShapeDtypeStruct((m, D_MODEL), F32),
        compiler_params=pltpu.CompilerParams(
            dimension_semantics=("parallel",), vmem_limit_bytes=VMEM_LIMIT),
        name="outproj",
    )(yh, yf, h, p, w_out, w_out, w_ple, w_gate, fnw)


def kernel(x, p, norm_w, w_in, fox_fb, hgrn_gn, hgrn_lb_logits, w_out, w_ple, w_ple_gate, final_norm_w):
    bsz, seq, _ = x.shape
    depth = w_in.shape[0]
    m = bsz * seq
    w_main = w_in.astype(BF16)
    w_fgate = jnp.pad(w_in[:, :, D_MAIN:], ((0, 0), (0, 0), (0, LANES - FOX_HEADS))).astype(BF16)
    fb = jnp.pad(fox_fb, ((0, 0), (0, LANES - FOX_HEADS)))
    w_out_b = w_out.astype(BF16)
    w_ple_b = w_ple.astype(BF16)
    w_gate_b = w_ple_gate.astype(BF16)
    fnw = final_norm_w.reshape(1, D_MODEL)

    p = p.reshape(depth, m, PLE_DIM)
    h = x.reshape(m, D_MODEL)
    for i in range(depth):
        proj, fz = _inproj(i, h, norm_w.reshape(depth, 1, D_MODEL), w_main, w_fgate)
        proj = proj.reshape(bsz, seq, D_MAIN)
        yh = _hgrn(i, proj, hgrn_lb_logits, hgrn_gn.reshape(depth, 1, D_HGRN))
        fa, fbm = _foxprep(i, fz.reshape(bsz, seq, LANES), fb.reshape(depth, 1, LANES))
        yf = _fox(proj, fa, fbm)
        h = _outproj(i, i == depth - 1, yh.reshape(m, D_HGRN), yf.reshape(m, D_FOX), h,
                     p, w_out_b, w_ple_b, w_gate_b, fnw)
    return h.reshape(bsz, seq, D_MODEL)
```

```python
import functools

import jax
import jax.numpy as jnp
from jax import lax
from jax.experimental import pallas as pl
from jax.experimental.pallas import tpu as pltpu

D_MODEL = 1024
PLE_DIM = 256
D_HGRN = 1024
D_FOX = 1024
HGRN_HEAD_DIM = 128
HGRN_HEADS = D_HGRN // HGRN_HEAD_DIM
FOX_HEAD_DIM = 64
FOX_HEADS = D_FOX // FOX_HEAD_DIM
NORM_EPS = 1e-6
D_MAIN = 4 * D_HGRN + 4 * D_FOX

LANES = 128
HGRN_SUB = 32
HGRN_SLAB = 128
HGRN_HEADS_PER_STEP = 4
FOX_TQ = 256
FOX_SLOT = 8
FOX_ONES_ROWS = 16
LOG2E = 1.4426950408889634
HGRN_FAST_MIN_LOG_DECAY = -60.0
VMEM_LIMIT = 48 * 1024 * 1024

F32 = jnp.float32
BF16 = jnp.bfloat16


def _dot(a, b):
    return jnp.dot(a, b, preferred_element_type=F32)


def _dot_nt(a, b):
    return lax.dot_general(a, b, (((1,), (1,)), ((), ())), preferred_element_type=F32)


def _split_bf16(x, n):
    parts = []
    r = x
    for _ in range(n - 1):
        hi = r.astype(BF16)
        parts.append(hi)
        r = r - hi.astype(F32)
    parts.append(r.astype(BF16))
    return parts


def _log_sigmoid(z):
    return jnp.minimum(z, 0.0) - jnp.log1p(jnp.exp(-jnp.abs(z)))


def _rmsnorm(x, w):
    return x * lax.rsqrt(jnp.mean(x * x, axis=-1, keepdims=True) + NORM_EPS) * w


def _inproj_kernel(x_ref, nw_ref, w_ref, wz_ref, proj_ref, fz_ref, u_ref):
    @pl.when(pl.program_id(1) == 0)
    def _():
        u = _rmsnorm(x_ref[...], nw_ref[...]).astype(BF16)
        u_ref[...] = u
        fz_ref[...] = _dot(u, wz_ref[...])

    proj_ref[...] = _dot(u_ref[...], w_ref[...]).astype(BF16)


def _inproj(layer, h, nw, w_main, w_gate, tm=1024, tn=4096):
    m = h.shape[0]
    return pl.pallas_call(
        _inproj_kernel,
        grid=(m // tm, D_MAIN // tn),
        in_specs=[
            pl.BlockSpec((tm, D_MODEL), lambda i, j: (i, 0)),
            pl.BlockSpec((None, 1, D_MODEL), lambda i, j: (layer, 0, 0)),
            pl.BlockSpec((None, D_MODEL, tn), lambda i, j: (layer, 0, j)),
            pl.BlockSpec((None, D_MODEL, LANES), lambda i, j: (layer, 0, 0)),
        ],
        out_specs=[
            pl.BlockSpec((tm, tn), lambda i, j: (i, j)),
            pl.BlockSpec((tm, LANES), lambda i, j: (i, 0)),
        ],
        out_shape=[
            jax.ShapeDtypeStruct((m, D_MAIN), BF16),
            jax.ShapeDtypeStruct((m, LANES), F32),
        ],
        scratch_shapes=[pltpu.VMEM((tm, D_MODEL), BF16)],
        compiler_params=pltpu.CompilerParams(
            dimension_semantics=("parallel", "arbitrary"), vmem_limit_bytes=VMEM_LIMIT),
        name="inproj",
    )(h, nw, w_main, w_gate)


def _hgrn_head(layer, ln, q_ref, f_ref, i_ref, g_ref, lbl_ref, gn_ref, o_ref,
               qf_ref, oi_ref, ds_ref, dec_ref, qs_ref, ks_ref, bs_ref, vs_ref, flag_ref):
    seq = q_ref.shape[1]
    n_slab = seq // HGRN_SLAB
    n_sub = HGRN_SLAB // HGRN_SUB
    n_chunk = seq // HGRN_SUB

    if layer > 0:
        logits = lbl_ref[:, ln]
        ex = jnp.exp(logits - jnp.max(logits, axis=0, keepdims=True))
        sm = ex / jnp.sum(ex, axis=0, keepdims=True)
        lb = jnp.sum(sm[1:layer + 1], axis=0, keepdims=True)
        log_lb = jnp.log(lb)
        log_1m_lb = jnp.log1p(-lb)
        one_m_lb = 1.0 - lb

    row = lax.broadcasted_iota(jnp.int32, (HGRN_SLAB, HGRN_SLAB), 0)
    col = lax.broadcasted_iota(jnp.int32, (HGRN_SLAB, HGRN_SLAB), 1)
    causal = (row >= col) & ((row // HGRN_SUB) == (col // HGRN_SUB))
    tril = jnp.where(causal, 1.0, 0.0).astype(BF16)
    row_chunk = lax.broadcasted_iota(jnp.int32, (HGRN_SLAB, HGRN_HEAD_DIM), 0) // HGRN_SUB
    scale = HGRN_HEAD_DIM ** -0.5
    gn = gn_ref[:, ln]

    def slab_rows(si):
        return pl.ds(pl.multiple_of(si * HGRN_SLAB, HGRN_SLAB), HGRN_SLAB)

    def decays(rows):
        fz = f_ref[0, rows, ln].astype(F32)
        e = jnp.exp(-jnp.abs(fz))
        r = 1.0 / (1.0 + e)
        log_sig = jnp.minimum(fz, 0.0) + jnp.log(r)
        sig_neg = jnp.where(fz >= 0.0, e * r, r)
        if layer > 0:
            bb = log_1m_lb + log_sig
            log_f = jnp.maximum(log_lb, bb) + jnp.log(1.0 + jnp.exp(-jnp.abs(log_lb - bb)))
            k = one_m_lb * sig_neg
        else:
            log_f = log_sig
            k = sig_neg
        hi, lo = _split_bf16(log_f * LOG2E, 2)
        csum = _dot(tril, jnp.concatenate([hi, lo], axis=1))
        return k, csum[:, :HGRN_HEAD_DIM] + csum[:, HGRN_HEAD_DIM:]

    def pass1_a(si):
        return decays(pl.ds(si * HGRN_SLAB, HGRN_SLAB))

    def pass1_b(si, k, b):
        rows = pl.ds(si * HGRN_SLAB, HGRN_SLAB)
        b_last = jnp.concatenate(
            [jnp.broadcast_to(b[(c + 1) * HGRN_SUB - 1:(c + 1) * HGRN_SUB, :], (HGRN_SUB, HGRN_HEAD_DIM))
             for c in range(n_sub)], axis=0)
        q = q_ref[0, rows, ln].astype(F32) * scale
        qf = (q * jnp.exp2(b)).astype(BF16)
        kf = (k * jnp.exp2(-b)).astype(BF16)
        kb = (k * jnp.exp2(b_last - b)).astype(BF16)
        qf_ref[rows, :] = qf
        return b, kb, _dot_nt(qf, kf)

    def pass1_c(si, b, kb, scores):
        rows = pl.ds(si * HGRN_SLAB, HGRN_SLAB)
        v = i_ref[0, rows, ln]
        oi_ref[rows, :] = _dot(jnp.where(causal, scores, 0.0).astype(BF16), v)
        v_t = v.astype(F32).T.astype(BF16)
        zero = jnp.zeros((), BF16)
        for c in range(0, n_sub, 2):
            kb2 = jnp.concatenate([jnp.where(row_chunk == c, kb, zero),
                                   jnp.where(row_chunk == c + 1, kb, zero)], axis=1)
            ds2 = _dot(v_t, kb2)
            ds_ref[si * n_sub + c] = ds2[:, :HGRN_HEAD_DIM]
            ds_ref[si * n_sub + c + 1] = ds2[:, HGRN_HEAD_DIM:]
        for c in range(n_sub):
            ci = si * n_sub + c
            dec_ref[pl.ds(ci, 1), :] = jnp.exp2(b[(c + 1) * HGRN_SUB - 1:(c + 1) * HGRN_SUB, :])
        flag_ref[si] = (jnp.min(b) < HGRN_FAST_MIN_LOG_DECAY * LOG2E).astype(jnp.int32)

    def redo_slab(si, carry):
        @pl.when(flag_ref[si] != 0)
        def _():
            rows = slab_rows(si)
            r0 = pl.multiple_of(si * HGRN_SLAB, HGRN_SLAB)
            k, b = decays(rows)
            qs_ref[...] = q_ref[0, rows, ln].astype(F32) * scale
            ks_ref[...] = k
            bs_ref[...] = b
            vs_ref[...] = i_ref[0, rows, ln].astype(F32)
            sub_row = lax.broadcasted_iota(jnp.int32, (HGRN_SUB, 1), 0)

            def one_row(t, c):
                c0 = pl.multiple_of((t // HGRN_SUB) * HGRN_SUB, HGRN_SUB)
                crow = pl.ds(c0, HGRN_SUB)
                d = bs_ref[pl.ds(t, 1), :] - bs_ref[crow, :]
                keep = (sub_row + c0) <= t
                w = jnp.where(keep, jnp.exp2(jnp.minimum(d, 0.0)), 0.0)
                sc = jnp.sum(qs_ref[pl.ds(t, 1), :] * ks_ref[crow, :] * w, axis=1, keepdims=True)
                oi_ref[pl.ds(r0 + t, 1), :] = jnp.sum(sc * vs_ref[crow, :], axis=0, keepdims=True)
                return c

            lax.fori_loop(0, HGRN_SLAB, one_row, 0)
        return carry

    def redo():
        lax.fori_loop(0, n_slab, redo_slab, 0)

    def pass2(ci, state_t):
        crow = pl.ds(ci * HGRN_SUB, HGRN_SUB)
        o = oi_ref[crow, :] + _dot_nt(qf_ref[crow, :], state_t.astype(BF16))
        o = o * lax.rsqrt(jnp.mean(o * o, axis=-1, keepdims=True) + NORM_EPS) * gn
        g = g_ref[0, crow, ln].astype(F32)
        o_ref[0, crow, ln] = (o * g * (1.0 / (1.0 + jnp.exp(-g)))).astype(BF16)
        return state_t * dec_ref[pl.ds(ci, 1), :] + ds_ref[ci]

    def pass1_stream():
        stage_a = {0: pass1_a(0)}
        stage_b = {}
        yield
        for step in range(n_slab + 1):
            if step + 1 < n_slab:
                stage_a[step + 1] = pass1_a(step + 1)
            if step < n_slab:
                stage_b[step] = pass1_b(step, *stage_a.pop(step))
            if step >= 1:
                pass1_c(step - 1, *stage_b.pop(step - 1))
            yield

    def pass2_stream():
        state_t = jnp.zeros((HGRN_HEAD_DIM, HGRN_HEAD_DIM), F32)
        for ci in range(n_chunk):
            state_t = pass2(ci, state_t)
            if ci % (n_chunk // (n_slab + 1) or 1) == 0:
                yield

    return pass1_stream, redo, pass2_stream


def _hgrn_kernel(layer, q_ref, f_ref, i_ref, g_ref, lbl_ref, gn_ref, o_ref, *scratch):
    nh = HGRN_HEADS_PER_STEP
    per = len(scratch) // nh
    heads = [_hgrn_head(layer, slice(h * HGRN_HEAD_DIM, (h + 1) * HGRN_HEAD_DIM),
                        q_ref, f_ref, i_ref, g_ref, lbl_ref, gn_ref, o_ref, *scratch[h * per:(h + 1) * per])
             for h in range(nh)]
    for _ in heads[0][0]():
        pass
    for h in range(nh):
        heads[h][1]()
        streams = [heads[h][2]()]
        if h + 1 < nh:
            streams.append(heads[h + 1][0]())
        while streams:
            streams = [g for g in streams if next(g, StopIteration) is not StopIteration]


def _hgrn(layer, proj, lb_logits, gn):
    bsz, seq, _ = proj.shape
    depth = lb_logits.shape[0]
    hps = HGRN_HEADS_PER_STEP
    groups = HGRN_HEADS // hps
    width = hps * HGRN_HEAD_DIM

    def col(j):
        return pl.BlockSpec((1, seq, width), lambda b, h: (b, 0, j * groups + h))

    slab = pltpu.VMEM((HGRN_SLAB, HGRN_HEAD_DIM), F32)
    return pl.pallas_call(
        functools.partial(_hgrn_kernel, layer),
        grid=(bsz, groups),
        in_specs=[col(0), col(1), col(2), col(3),
                  pl.BlockSpec((depth, width), lambda b, h: (0, h)),
                  pl.BlockSpec((None, 1, width), lambda b, h: (layer, 0, h))],
        out_specs=pl.BlockSpec((1, seq, width), lambda b, h: (b, 0, h)),
        out_shape=jax.ShapeDtypeStruct((bsz, seq, D_HGRN), BF16),
        scratch_shapes=hps * [
            pltpu.VMEM((seq, HGRN_HEAD_DIM), BF16),
            pltpu.VMEM((seq, HGRN_HEAD_DIM), F32),
            pltpu.VMEM((seq // HGRN_SUB, HGRN_HEAD_DIM, HGRN_HEAD_DIM), F32),
            pltpu.VMEM((seq // HGRN_SUB, HGRN_HEAD_DIM), F32),
            slab, slab, slab, slab,
            pltpu.SMEM((seq // HGRN_SLAB,), jnp.int32),
        ],
        compiler_params=pltpu.CompilerParams(
            dimension_semantics=("parallel", "parallel"), vmem_limit_bytes=VMEM_LIMIT),
        name="hgrn2",
    )(proj, proj, proj, proj, lb_logits, gn)


def _foxprep_kernel(fz_ref, fb_ref, fa_ref, fbm_ref):
    seq = fz_ref.shape[1]
    blk = LANES
    n_blk = seq // blk
    row = lax.broadcasted_iota(jnp.int32, (blk, blk), 0)
    col = lax.broadcasted_iota(jnp.int32, (blk, blk), 1)
    tril = jnp.where(row >= col, 1.0, 0.0).astype(BF16)

    def place(j, sign):
        return jnp.where((col == FOX_SLOT * row + j) & (row < FOX_HEADS), sign, 0.0).astype(BF16)

    sel = jnp.concatenate(
        [jnp.concatenate([place(j, 1.0), place(3 + j, -1.0)], axis=1) for j in range(3)], axis=0)
    lane = lax.broadcasted_iota(jnp.int32, (1, blk), 1) % FOX_SLOT
    ones = jnp.concatenate([jnp.where((lane >= 3) & (lane < 6), 1.0, 0.0), jnp.where(lane < 3, 1.0, 0.0)], axis=1)
    fb = fb_ref[...]

    def prefix(i):
        log_f = _log_sigmoid(fz_ref[0, i * blk:(i + 1) * blk, :] + fb)
        t = _dot(tril, jnp.concatenate(_split_bf16(log_f, 3), axis=1))
        return t[:, :blk] + t[:, blk:2 * blk] + t[:, 2 * blk:]

    local = [prefix(i) for i in range(n_blk)]
    carry = jnp.zeros((1, blk), F32)
    for i in range(n_blk):
        rows = slice(i * blk, (i + 1) * blk)
        csum = local[i] + carry
        carry = carry + local[i][blk - 1:blk, :]
        packed = ones + _dot(jnp.concatenate(_split_bf16(csum * LOG2E, 3), axis=1), sel)
        fa_ref[0, rows, :] = packed[:, :blk].astype(BF16)
        fbm_ref[0, rows, :] = packed[:, blk:].astype(BF16)


def _foxprep(layer, fz, fb):
    bsz, seq, _ = fz.shape
    spec = pl.BlockSpec((1, seq, LANES), lambda b: (b, 0, 0))
    return pl.pallas_call(
        _foxprep_kernel,
        grid=(bsz,),
        in_specs=[spec, pl.BlockSpec((None, 1, LANES), lambda b: (layer, 0, 0))],
        out_specs=[spec, spec],
        out_shape=[jax.ShapeDtypeStruct((bsz, seq, LANES), BF16)] * 2,
        compiler_params=pltpu.CompilerParams(
            dimension_semantics=("parallel",), vmem_limit_bytes=VMEM_LIMIT),
        name="foxprep",
    )(fz, fb)


def _fox_kernel(q_ref, k_ref, v_ref, g_ref, fa_ref, fbm_ref, o_ref, ka_ref, vt_ref,
                s0_ref, s1_ref, p0_ref, p1_ref):
    seq = q_ref.shape[1]
    tq = FOX_TQ
    pair = pl.program_id(1)
    lane = lax.broadcasted_iota(jnp.int32, (1, LANES), 1)
    key_i = lax.broadcasted_iota(jnp.int32, (tq, tq), 0)
    qry_i = lax.broadcasted_iota(jnp.int32, (tq, tq), 1)
    diag_keep = key_i <= qry_i
    keep2 = jnp.concatenate([diag_keep, diag_keep], axis=1)
    neg = -1e30
    qscale = FOX_HEAD_DIM ** -0.5 * LOG2E
    zero = jnp.zeros((), BF16)
    s_bufs = (s0_ref, s1_ref)
    p_bufs = (p0_ref, p1_ref)

    def stage():
        ka_ref[:, :LANES] = k_ref[0]
        ka_ref[:, LANES:] = fbm_ref[0]
        v_t = v_ref[0].astype(F32).T.astype(BF16)
        for hh in range(2):
            vt_ref[hh, :FOX_HEAD_DIM, :] = v_t[hh * FOX_HEAD_DIM:(hh + 1) * FOX_HEAD_DIM]
            vt_ref[hh, FOX_HEAD_DIM:, :] = jnp.ones((FOX_ONES_ROWS, seq), BF16)

    def scores(j):
        rows = slice(j * tq, (j + 1) * tq)
        n_off = j * tq
        q = (q_ref[0, rows, :].astype(F32) * qscale).astype(BF16)
        fa = fa_ref[0, rows, :]
        qa = []
        for hh in range(2):
            hmask = (lane // FOX_HEAD_DIM) == hh
            fmask = (lane // FOX_SLOT) == (2 * pair + hh)
            qa.append(jnp.concatenate(
                [jnp.where(hmask, q, zero), jnp.where(fmask, fa, zero)], axis=1))
        qa = jnp.concatenate(qa, axis=0)
        s_buf = s_bufs[j % 2]
        s_diag = jnp.where(keep2, _dot_nt(ka_ref[rows, :], qa), neg)
        s_buf[rows, :] = s_diag
        m = jnp.max(s_diag, axis=0, keepdims=True)
        if n_off:
            s_off = _dot_nt(ka_ref[:n_off, :], qa)
            s_buf[:n_off, :] = s_off
            m = jnp.maximum(m, jnp.max(s_off, axis=0, keepdims=True))
        return j, m

    def probs(j, m):
        for t in range(j + 1):
            keys = slice(t * tq, (t + 1) * tq)
            p_bufs[j % 2][keys, :] = jnp.exp2(s_bufs[j % 2][keys, :] - m).astype(BF16)
            yield

    def weighted_values(j):
        rows = slice(j * tq, (j + 1) * tq)
        acc = [None, None]
        for t in range(j + 1):
            keys = slice(t * tq, (t + 1) * tq)
            for hh in range(2):
                qs = slice(hh * tq, (hh + 1) * tq)
                part = _dot(vt_ref[hh, :, keys], p_bufs[j % 2][keys, qs])
                acc[hh] = part if acc[hh] is None else acc[hh] + part
            yield
        outs = [a[:FOX_HEAD_DIM] * (1.0 / a[FOX_HEAD_DIM:FOX_HEAD_DIM + 1]) for a in acc]
        o = jnp.concatenate(outs, axis=0).T
        g = g_ref[0, rows, :].astype(F32)
        o_ref[0, rows, :] = (o * g * (1.0 / (1.0 + jnp.exp(-g)))).astype(BF16)
        yield

    stage()
    n_blk = seq // tq
    s_next = scores(0)
    for j in range(n_blk + 1):
        s_cur, s_next = s_next, (scores(j + 1) if j + 1 < n_blk else None)
        streams = []
        if j < n_blk:
            streams.append(probs(*s_cur))
        if j > 0:
            streams.append(weighted_values(j - 1))
        while streams:
            streams = [g for g in streams if next(g, StopIteration) is not StopIteration]


def _fox(proj, fa, fbm):
    bsz, seq, _ = proj.shape
    npair = D_FOX // LANES
    base = 4 * D_HGRN // LANES

    def col(j):
        return pl.BlockSpec((1, seq, LANES), lambda b, h: (b, 0, base + j * npair + h))

    gate = pl.BlockSpec((1, seq, LANES), lambda b, h: (b, 0, 0))
    return pl.pallas_call(
        _fox_kernel,
        grid=(bsz, npair),
        in_specs=[col(0), col(1), col(2), col(3), gate, gate],
        out_specs=pl.BlockSpec((1, seq, LANES), lambda b, h: (b, 0, h)),
        out_shape=jax.ShapeDtypeStruct((bsz, seq, D_FOX), BF16),
        scratch_shapes=[pltpu.VMEM((seq, 2 * LANES), BF16),
                        pltpu.VMEM((2, FOX_HEAD_DIM + FOX_ONES_ROWS, seq), BF16),
                        pltpu.VMEM((seq, 2 * FOX_TQ), F32), pltpu.VMEM((seq, 2 * FOX_TQ), F32),
                        pltpu.VMEM((seq, 2 * FOX_TQ), BF16), pltpu.VMEM((seq, 2 * FOX_TQ), BF16)],
        compiler_params=pltpu.CompilerParams(
            dimension_semantics=("parallel", "parallel"), vmem_limit_bytes=VMEM_LIMIT),
        name="fox",
    )(proj, proj, proj, proj, fa, fbm)


def _outproj_kernel(final, yh_ref, yf_ref, h_ref, p_ref, woh_ref, wof_ref, wp_ref, wg_ref,
                    fnw_ref, o_ref):
    h = h_ref[...] + _dot(yh_ref[...], woh_ref[...]) + _dot(yf_ref[...], wof_ref[...])
    ple = _dot(p_ref[...].astype(BF16), wp_ref[...])
    gate = _dot(h.astype(BF16), wg_ref[...])
    h = h + (1.0 / (1.0 + jnp.exp(-gate))) * ple
    if final:
        h = _rmsnorm(h, fnw_ref[...])
    o_ref[...] = h


def _outproj(layer, final, yh, yf, h, p, w_out, w_ple, w_gate, fnw, tm=1024):
    m = h.shape[0]

    def rows(width):
        return pl.BlockSpec((tm, width), lambda i: (i, 0))

    def weight(k, blk=0):
        return pl.BlockSpec((None, k, D_MODEL), lambda i: (layer, blk, 0))

    return pl.pallas_call(
        functools.partial(_outproj_kernel, final),
        grid=(m // tm,),
        in_specs=[rows(D_HGRN), rows(D_FOX), rows(D_MODEL),
                  pl.BlockSpec((None, tm, PLE_DIM), lambda i: (layer, i, 0)),
                  weight(D_HGRN, 0), weight(D_FOX, 1), weight(PLE_DIM), weight(D_MODEL),
                  pl.BlockSpec((1, D_MODEL), lambda i: (0, 0))],
        out_specs=rows(D_MODEL),
        out_shape=jax.ShapeDtypeStruct((m, D_MODEL), F32),
        compiler_params=pltpu.CompilerParams(
            dimension_semantics=("parallel",), vmem_limit_bytes=VMEM_LIMIT),
        name="outproj",
    )(yh, yf, h, p, w_out, w_out, w_ple, w_gate, fnw)


def kernel(x, p, norm_w, w_in, fox_fb, hgrn_gn, hgrn_lb_logits, w_out, w_ple, w_ple_gate, final_norm_w):
    bsz, seq, _ = x.shape
    depth = w_in.shape[0]
    m = bsz * seq
    w_main = w_in.astype(BF16)
    w_fgate = jnp.pad(w_in[:, :, D_MAIN:], ((0, 0), (0, 0), (0, LANES - FOX_HEADS))).astype(BF16)
    fb = jnp.pad(fox_fb, ((0, 0), (0, LANES - FOX_HEADS)))
    w_out_b = w_out.astype(BF16)
    w_ple_b = w_ple.astype(BF16)
    w_gate_b = w_ple_gate.astype(BF16)
    fnw = final_norm_w.reshape(1, D_MODEL)

    p = p.reshape(depth, m, PLE_DIM)
    h = x.reshape(m, D_MODEL)
    for i in range(depth):
        proj, fz = _inproj(i, h, norm_w.reshape(depth, 1, D_MODEL), w_main, w_fgate)
        proj = proj.reshape(bsz, seq, D_MAIN)
        yh = _hgrn(i, proj, hgrn_lb_logits, hgrn_gn.reshape(depth, 1, D_HGRN))
        fa, fbm = _foxprep(i, fz.reshape(bsz, seq, LANES), fb.reshape(depth, 1, LANES))
        yf = _fox(proj, fa, fbm)
        h = _outproj(i, i == depth - 1, yh.reshape(m, D_HGRN), yf.reshape(m, D_FOX), h,
                     p, w_out_b, w_ple_b, w_gate_b, fnw)
    return h.reshape(bsz, seq, D_MODEL)
```
